```python
import math
import functools
import jax
import jax.numpy as jnp
from jax import lax
import numpy as np

D_MODEL = 2048
BATCH = 4
SEQ = 2048
DEPTH = 2
DEC_BATCH = 128
DEC_SEQ = 8
PAST_LEN = 2048
PAGE_SIZE = 128

H_A = 8
DK_A = 128
DV_A = 128
W_A = H_A * DV_A
CHUNK_A = 64
H_B = 8
DH_B = 128
W_B = H_B * DH_B
MOBA_BLOCK = 256
MOBA_TOPK = 3
Q_CHUNK_B = 32
W_C = 1024
CONV_W = 3
N_BRANCH = 3
N_EXPERTS = 32
TOP_K = 4
D_FF = D_MODEL
SWIGLU_LIMIT = 7.0
SWIGLU_ALPHA = 1.702
MOE_BLOCK = 128
EPS = 1e-6
SPLIT_SIZES = (H_A * DK_A, H_A * DK_A, W_A, W_A, W_B, W_B, W_B, W_C, W_C, W_C, N_BRANCH * D_MODEL)
D_IN = sum(SPLIT_SIZES)
SPLIT_AT = tuple(int(s) for s in np.cumsum(SPLIT_SIZES)[:-1])

kernel_name = 'hybrid_hgrn2_moba_shortconv_moe_step'


def rmsnorm(x, g):
    xf = x.astype(jnp.float32)
    y = xf * lax.rsqrt(jnp.mean(xf * xf, axis=-1, keepdims=True) + EPS)
    return (y * g.astype(jnp.float32)).astype(x.dtype)


def alibi_slopes(n_heads):
    e = (np.arange(n_heads, dtype=np.float32) + 1.0) * (-8.0 / n_heads)
    return jnp.asarray(np.exp2(e), dtype=jnp.float32)


def hgrn2_scan(q, log_f, k, i, s0):
    B, T = q.shape[0], q.shape[1]
    L = math.gcd(T, CHUNK_A)
    nc = T // L

    def to_chunks(a):
        return a.reshape(B, nc, L, a.shape[2], a.shape[3]).transpose(1, 0, 3, 2, 4)

    causal = jnp.tril(jnp.ones((L, L), dtype=bool))[None, None, :, :, None]

    def step(S, inp):
        qc, lfc, kc, ic = inp
        b = jnp.cumsum(lfc, axis=2)
        o_inter = jnp.einsum('bhtk,bhkv->bhtv', qc * jnp.exp(b), S)
        rel = jnp.where(causal, b[:, :, :, None, :] - b[:, :, None, :, :], -jnp.inf)
        att = jnp.einsum('bhtk,bhtsk,bhsk->bhts', qc, jnp.exp(rel), kc)
        o_intra = jnp.einsum('bhts,bhsv->bhtv', att, ic)
        b_end = b[:, :, -1, :]
        k_dec = kc * jnp.exp(b_end[:, :, None, :] - b)
        S_new = jnp.exp(b_end)[..., None] * S + jnp.einsum('bhsk,bhsv->bhkv', k_dec, ic)
        return S_new, o_inter + o_intra

    s_fin, o = lax.scan(step, s0, (to_chunks(q), to_chunks(log_f), to_chunks(k), to_chunks(i)))
    o = o.transpose(1, 0, 3, 2, 4).reshape(B, T, H_A, DV_A)
    return o, s_fin


def moba_seq(q, q_pos, k, v, slopes):
    tq, lk = q.shape[0], k.shape[0]
    nb = -(-lk // MOBA_BLOCK)
    pad = nb * MOBA_BLOCK - lk
    kb = jnp.pad(k, ((0, pad), (0, 0), (0, 0))).reshape(nb, MOBA_BLOCK, H_B, DH_B).transpose(2, 0, 1, 3)
    vb = jnp.pad(v, ((0, pad), (0, 0), (0, 0))).reshape(nb, MOBA_BLOCK, H_B, DH_B).transpose(2, 0, 1, 3)
    cnt = jnp.asarray(np.clip(lk - np.arange(nb) * MOBA_BLOCK, 1, MOBA_BLOCK).astype(np.float32))
    k_mean = kb.astype(jnp.float32).sum(axis=2) / cnt[None, :, None]
    q32 = q.astype(jnp.float32)
    q_blk = q_pos // MOBA_BLOCK
    gate = jnp.einsum('thd,hnd->htn', q32, k_mean)
    past = jnp.arange(nb)[None, :] < q_blk[:, None]
    gate = jnp.where(past[None], gate, -jnp.inf)
    top = min(MOBA_TOPK, nb)
    g_val, g_idx = lax.top_k(gate, top)
    g_ok = jnp.isfinite(g_val)
    qc = math.gcd(tq, Q_CHUNK_B)
    nq = tq // qc
    offs = jnp.arange(MOBA_BLOCK, dtype=jnp.int32)
    h_ix = jnp.arange(H_B)[:, None, None]
    scale = DH_B ** -0.5

    def chunk(args):
        qq, pp, ii, oo = args
        own = pp // MOBA_BLOCK
        keys = jnp.concatenate([kb[h_ix, ii].reshape(H_B, qc, top * MOBA_BLOCK, DH_B), kb[:, own]], axis=2)
        vals = jnp.concatenate([vb[h_ix, ii].reshape(H_B, qc, top * MOBA_BLOCK, DH_B), vb[:, own]], axis=2)
        pos_sel = (ii[..., None] * MOBA_BLOCK + offs).reshape(H_B, qc, top * MOBA_BLOCK)
        pos_own = jnp.broadcast_to(own[:, None] * MOBA_BLOCK + offs, (H_B, qc, MOBA_BLOCK))
        k_pos = jnp.concatenate([pos_sel, pos_own], axis=2)
        ok = jnp.concatenate([
            jnp.broadcast_to(oo[..., None], (H_B, qc, top, MOBA_BLOCK)).reshape(H_B, qc, top * MOBA_BLOCK),
            pos_own <= pp[None, :, None]], axis=2)
        s = jnp.einsum('qhd,hqkd->hqk', qq * scale, keys.astype(jnp.float32))
        s = s - slopes[:, None, None] * (pp[None, :, None] - k_pos).astype(jnp.float32)
        p = jax.nn.softmax(jnp.where(ok, s, -jnp.inf), axis=-1)
        return jnp.einsum('hqk,hqkd->qhd', p, vals.astype(jnp.float32))

    out = lax.map(chunk, (q32.reshape(nq, qc, H_B, DH_B), q_pos.reshape(nq, qc),
                          g_idx.reshape(H_B, nq, qc, top).transpose(1, 0, 2, 3),
                          g_ok.reshape(H_B, nq, qc, top).transpose(1, 0, 2, 3)))
    return out.reshape(tq, H_B, DH_B).astype(q.dtype)


def moba_prompt(q, k, v):
    pos = jnp.arange(q.shape[1], dtype=jnp.int32)
    slopes = alibi_slopes(H_B)
    return lax.map(lambda a: moba_seq(a[0], pos, a[1], a[2], slopes), (q, k, v))


def moba_sample(q, k, v, cache_k, cache_v, page_table, layer):
    tq = q.shape[1]
    past_len = page_table.shape[1] * cache_k.shape[2]
    pos = past_len + jnp.arange(tq, dtype=jnp.int32)
    slopes = alibi_slopes(H_B)

    def one(a):
        qs, kn, vn, pt = a
        kp = cache_k[layer, pt].reshape(past_len, H_B, DH_B)
        vp = cache_v[layer, pt].reshape(past_len, H_B, DH_B)
        k_all = jnp.concatenate([kp, kn.astype(kp.dtype)], axis=0)
        v_all = jnp.concatenate([vp, vn.astype(vp.dtype)], axis=0)
        return moba_seq(qs, pos, k_all, v_all, slopes)

    return lax.map(one, (q, k, v, page_table))


def short_conv(u, buf, w):
    ext = jnp.concatenate([buf.astype(u.dtype), u], axis=1)
    y = lax.conv_general_dilated(ext, w[:, None, :].astype(u.dtype), window_strides=(1,), padding='VALID',
                                 dimension_numbers=('NWC', 'WIO', 'NWC'), feature_group_count=u.shape[-1])
    return y, ext[:, ext.shape[1] - (CONV_W - 1):]


def token_mixers(h, lb, w_in, g_hgrn_out, conv_w, w_br_a, w_br_b, w_br_c, w_o, s0, buf0, attend):
    B, T, _ = h.shape
    f32 = jnp.float32
    qa, fa, ia, ga, qb, kb, vb, bc, cc, hc, gl = jnp.split(h @ w_in, SPLIT_AT, axis=-1)
    lbh = lb.reshape(H_A, DK_A)
    fr = fa.astype(f32).reshape(B, T, H_A, DK_A)
    forget = lbh + (1.0 - lbh) * jax.nn.sigmoid(fr)
    k_in = (1.0 - lbh) * jax.nn.sigmoid(-fr)
    o_a, s_new = hgrn2_scan(qa.astype(f32).reshape(B, T, H_A, DK_A), jnp.log(forget), k_in,
                            ia.astype(f32).reshape(B, T, H_A, DV_A), s0.astype(f32))
    o_a = o_a * lax.rsqrt(jnp.mean(o_a * o_a, axis=-1, keepdims=True) + EPS) * g_hgrn_out.astype(f32).reshape(H_A, DV_A)
    o_a = (o_a * jax.nn.silu(ga.astype(f32).reshape(B, T, H_A, DV_A))).reshape(B, T, W_A).astype(h.dtype)
    k4 = kb.reshape(B, T, H_B, DH_B)
    v4 = vb.reshape(B, T, H_B, DH_B)
    o_b = attend(qb.reshape(B, T, H_B, DH_B), k4, v4).reshape(B, T, W_B)
    y_c, buf_new = short_conv(cc * hc, buf0, conv_w)
    o_c = bc * y_c
    g_a, g_b, g_c = jnp.split(jax.nn.sigmoid(gl), N_BRANCH, axis=-1)
    merged = g_a * (o_a @ w_br_a) + g_b * (o_b @ w_br_b) + g_c * (o_c @ w_br_c)
    return merged @ w_o, s_new.astype(h.dtype), buf_new, k4, v4


def moe_ffn(h, layer, w_router, b_router, w_gu, b_gu, w_dn, b_dn):
    B, T, D = h.shape
    n = B * T
    xf = h.reshape(n, D)
    logits = (xf @ w_router[layer] + b_router[layer]).astype(jnp.float32)
    top_v, top_e = lax.top_k(logits, TOP_K)
    top_w = jax.nn.softmax(top_v, axis=-1)
    e_flat = top_e.reshape(-1)
    order = jnp.argsort(e_flat)
    e_s = e_flat[order]
    t_s = (order // TOP_K).astype(jnp.int32)
    w_s = top_w.reshape(-1)[order]
    counts = jnp.zeros((N_EXPERTS,), jnp.int32).at[e_flat].add(1)
    start = jnp.cumsum(counts) - counts
    padded = (counts + MOE_BLOCK - 1) // MOE_BLOCK * MOE_BLOCK
    p_end = jnp.cumsum(padded)
    dest = (p_end - padded)[e_s] + jnp.arange(n * TOP_K, dtype=jnp.int32) - start[e_s]
    n_blk = -(-(n * TOP_K) // MOE_BLOCK) + N_EXPERTS
    rows = n_blk * MOE_BLOCK
    row_tok = jnp.full((rows,), n, jnp.int32).at[dest].set(t_s)
    row_w = jnp.zeros((rows,), jnp.float32).at[dest].set(w_s)
    blk_e = jnp.minimum(jnp.searchsorted(p_end, jnp.arange(n_blk, dtype=jnp.int32) * MOE_BLOCK, side='right'),
                        N_EXPERTS - 1)
    x_pad = jnp.concatenate([xf, jnp.zeros((1, D), xf.dtype)], axis=0)

    def expert_block(args):
        tok, e = args
        gu = x_pad[tok] @ w_gu[layer, e] + b_gu[layer, e]
        gate = jnp.minimum(gu[:, :D_FF], SWIGLU_LIMIT)
        lin = jnp.clip(gu[:, D_FF:], -SWIGLU_LIMIT, SWIGLU_LIMIT)
        act = gate * jax.nn.sigmoid(SWIGLU_ALPHA * gate) * (lin + 1.0)
        return act @ w_dn[layer, e] + b_dn[layer, e]

    y = lax.map(expert_block, (row_tok.reshape(n_blk, MOE_BLOCK), blk_e))
    out = jnp.zeros((n + 1, D), jnp.float32).at[row_tok].add(row_w[:, None] * y.reshape(rows, D).astype(jnp.float32))
    return out[:n].reshape(B, T, D).astype(h.dtype)


def trunk(x, c, hgrn_init, conv_init, attend_fns, w_ada, b_ada, g_norm_mix, w_in, hgrn_lb_logits, g_hgrn_out,
          conv_w, w_br_a, w_br_b, w_br_c, w_o, g_norm_ffn, w_router, b_router, w_gu, b_gu, w_dn, b_dn, g_final):
    lb_soft = jax.nn.softmax(hgrn_lb_logits.astype(jnp.float32), axis=0)
    lower_bounds = jnp.cumsum(lb_soft, axis=0) - lb_soft[0:1]
    c_act = jax.nn.silu(c)
    ks, vs, ss, bs = [], [], [], []
    for l in range(DEPTH):
        mod = (c_act @ w_ada[l] + b_ada[l])[:, None, :]
        sh1, sc1, gt1, sh2, sc2, gt2 = jnp.split(mod, 6, axis=-1)
        h = rmsnorm(x, g_norm_mix[l]) * (1.0 + sc1) + sh1
        mix, s_new, b_new, k_l, v_l = token_mixers(h, lower_bounds[l], w_in[l], g_hgrn_out[l], conv_w[l], w_br_a[l],
                                                   w_br_b[l], w_br_c[l], w_o[l], hgrn_init[l], conv_init[l],
                                                   attend_fns[l])
        x = x + gt1 * mix
        h = rmsnorm(x, g_norm_ffn[l]) * (1.0 + sc2) + sh2
        x = x + gt2 * moe_ffn(h, l, w_router, b_router, w_gu, b_gu, w_dn, b_dn)
        ks.append(k_l)
        vs.append(v_l)
        ss.append(s_new)
        bs.append(b_new)
    return rmsnorm(x, g_final), jnp.stack(ks), jnp.stack(vs), jnp.stack(ss), jnp.stack(bs)


def setup_inputs(seed: int = 0) -> dict:
    key = jax.random.key(seed)
    ks = jax.random.split(key, 32)
    f32 = jnp.float32
    n_pages = PAST_LEN // PAGE_SIZE
    n_pool = (DEC_BATCH * n_pages * 5) // 4

    def nrm(k, shape, scale):
        return jax.random.normal(k, shape, f32) * scale

    return {
        'x_prompt': nrm(ks[0], (BATCH, SEQ, D_MODEL), 1.0),
        'x_sample': nrm(ks[1], (DEC_BATCH, DEC_SEQ, D_MODEL), 1.0),
        'cache_k': nrm(ks[2], (DEPTH, n_pool, PAGE_SIZE, H_B, DH_B), 1.0),
        'cache_v': nrm(ks[3], (DEPTH, n_pool, PAGE_SIZE, H_B, DH_B), 1.0),
        'state_hgrn': nrm(ks[4], (DEPTH, DEC_BATCH, H_A, DK_A, DV_A), 0.5),
        'state_conv': nrm(ks[5], (DEPTH, DEC_BATCH, CONV_W - 1, W_C), 1.0),
        'page_table': jax.random.permutation(ks[6], n_pool)[:DEC_BATCH * n_pages].reshape(DEC_BATCH, n_pages).astype(jnp.int32),
        'c_prompt': nrm(ks[7], (BATCH, D_MODEL), 1.0),
        'c_sample': nrm(ks[8], (DEC_BATCH, D_MODEL), 1.0),
        'w_ada': nrm(ks[9], (DEPTH, D_MODEL, 6 * D_MODEL), 0.5 * D_MODEL ** -0.5),
        'b_ada': nrm(ks[10], (DEPTH, 6 * D_MODEL), 0.02),
        'g_norm_mix': 1.0 + nrm(ks[11], (DEPTH, D_MODEL), 0.02),
        'w_in': nrm(ks[12], (DEPTH, D_MODEL, D_IN), D_MODEL ** -0.5),
        'hgrn_lb_logits': nrm(ks[13], (DEPTH, H_A * DK_A), 1.0),
        'g_hgrn_out': 1.0 + nrm(ks[14], (DEPTH, W_A), 0.02),
        'conv_w': nrm(ks[15], (DEPTH, CONV_W, W_C), CONV_W ** -0.5),
        'w_br_a': nrm(ks[16], (DEPTH, W_A, D_MODEL), W_A ** -0.5),
        'w_br_b': nrm(ks[17], (DEPTH, W_B, D_MODEL), W_B ** -0.5),
        'w_br_c': nrm(ks[18], (DEPTH, W_C, D_MODEL), W_C ** -0.5),
        'w_o': nrm(ks[19], (DEPTH, D_MODEL, D_MODEL), D_MODEL ** -0.5),
        'g_norm_ffn': 1.0 + nrm(ks[20], (DEPTH, D_MODEL), 0.02),
        'w_router': nrm(ks[21], (DEPTH, D_MODEL, N_EXPERTS), D_MODEL ** -0.5),
        'b_router': nrm(ks[22], (DEPTH, N_EXPERTS), 0.01),
        'w_gu': nrm(ks[23], (DEPTH, N_EXPERTS, D_MODEL, 2 * D_FF), D_MODEL ** -0.5),
        'b_gu': nrm(ks[24], (DEPTH, N_EXPERTS, 2 * D_FF), 0.01),
        'w_dn': nrm(ks[25], (DEPTH, N_EXPERTS, D_FF, D_MODEL), D_FF ** -0.5),
        'b_dn': nrm(ks[26], (DEPTH, N_EXPERTS, D_MODEL), 0.01),
        'g_final': 1.0 + nrm(ks[27], (D_MODEL,), 0.02),
    }


def reference(x_prompt, x_sample, cache_k, cache_v, state_hgrn, state_conv, page_table, c_prompt, c_sample,
              w_ada, b_ada, g_norm_mix, w_in, hgrn_lb_logits, g_hgrn_out, conv_w, w_br_a, w_br_b, w_br_c, w_o,
              g_norm_ffn, w_router, b_router, w_gu, b_gu, w_dn, b_dn, g_final):
    weights = (w_ada, b_ada, g_norm_mix, w_in, hgrn_lb_logits, g_hgrn_out, conv_w, w_br_a, w_br_b, w_br_c, w_o,
               g_norm_ffn, w_router, b_router, w_gu, b_gu, w_dn, b_dn, g_final)
    b_p = x_prompt.shape[0]
    hgrn0 = jnp.zeros((DEPTH, b_p, H_A, DK_A, DV_A), x_prompt.dtype)
    conv0 = jnp.zeros((DEPTH, b_p, CONV_W - 1, W_C), x_prompt.dtype)
    y_prompt, k_prompt, v_prompt, hgrn_prompt, conv_prompt = trunk(
        x_prompt, c_prompt, hgrn0, conv0, [moba_prompt] * DEPTH, *weights)
    sample_fns = [functools.partial(moba_sample, cache_k=cache_k, cache_v=cache_v, page_table=page_table, layer=l)
                  for l in range(DEPTH)]
    y_sample, k_sample, v_sample, hgrn_sample, conv_sample = trunk(
        x_sample, c_sample, state_hgrn, state_conv, sample_fns, *weights)
    return (y_prompt, y_sample, k_prompt, v_prompt, k_sample, v_sample, hgrn_prompt, hgrn_sample, conv_prompt, conv_sample)
```

```python
import functools
import math

import jax
import jax.numpy as jnp
import numpy as np
from jax import lax
from jax.experimental import pallas as pl
from jax.experimental.pallas import tpu as pltpu

F32 = jnp.float32
BF16 = jnp.bfloat16

SUBLANES = 8
LANES = 128

EPS = 1e-6
MOBA_BLOCK = 256
MOBA_TOPK = 3
TOP_K = 4
SWIGLU_LIMIT = 7.0
SWIGLU_ALPHA = 1.702
MOE_BLOCK = 128
CONV_W = 3

HGRN_CHUNK = 16
NORM_TM = 256
MOE_GROUP_BLOCKS = 10
MOE_FF_TILE = 256
VMEM_LIMIT = 56 * 1024 * 1024

NT_DIMS = (((1,), (1,)), ((), ()))
TN_DIMS = (((0,), (0,)), ((), ()))


def _cparams(sem):
    return pltpu.CompilerParams(dimension_semantics=sem, vmem_limit_bytes=VMEM_LIMIT)


def _pick(n, cands):
    for c in cands:
        if n % c == 0:
            return c
    raise ValueError(f"no tile in {cands} divides {n}")


def _layer_vec(v):
    return v.reshape(v.shape[0], 1, v.shape[1])


def _layer_vec_spec(layer, n):
    return pl.BlockSpec((None, 1, n), lambda *_: (layer, 0, 0))


def _split_bf16(x):
    hi = x.astype(BF16)
    lo = (x - hi.astype(F32)).astype(BF16)
    return hi, lo


def _dot3(a, b, dims):
    ah, al = _split_bf16(a)
    bh, bl = _split_bf16(b)
    dg = functools.partial(lax.dot_general, dimension_numbers=dims, preferred_element_type=F32)
    return dg(ah, bh) + dg(al, bh) + dg(ah, bl)


def _ada_kernel(c_ref, w_ref, b_ref, o_ref):
    c = c_ref[...]
    ca = (c * jax.nn.sigmoid(c)).astype(BF16)
    o_ref[0] = jnp.dot(ca, w_ref[0].astype(BF16), preferred_element_type=F32) + b_ref[0]


def _ada(c_rows, w_ada, b_ada):
    depth, d, n6 = w_ada.shape
    r = c_rows.shape[0]
    tn = _pick(n6, (1024, 512, 256, 128))
    return pl.pallas_call(
        _ada_kernel,
        grid=(depth, n6 // tn),
        in_specs=[
            pl.BlockSpec((r, d), lambda l, j: (0, 0)),
            pl.BlockSpec((1, d, tn), lambda l, j: (l, 0, j)),
            pl.BlockSpec((1, 1, tn), lambda l, j: (l, 0, j)),
        ],
        out_specs=pl.BlockSpec((1, r, tn), lambda l, j: (l, 0, j)),
        out_shape=jax.ShapeDtypeStruct((depth, r, n6), F32),
        compiler_params=_cparams(("arbitrary", "arbitrary")),
        name="ada_mod",
    )(c_rows, w_ada, b_ada.reshape(depth, 1, n6))


def _rms(x, g):
    ms = jnp.mean(x * x, axis=-1, keepdims=True)
    return x * lax.rsqrt(ms + EPS) * g


def _modulate(y, sc, sh):
    tm, d = y.shape
    y3 = y.reshape(tm // SUBLANES, SUBLANES, d)
    return (y3 * (1.0 + sc) + sh).reshape(tm, d)


def _norm_kernel(x_ref, g_ref, sc_ref, sh_ref, o_ref):
    h = _modulate(_rms(x_ref[...], g_ref[...]), sc_ref[...], sh_ref[...])
    o_ref[...] = h.astype(o_ref.dtype)


def _norm_router_kernel(x_ref, g_ref, sc_ref, sh_ref, wr_ref, br_ref, h_ref, lg_ref):
    h = _modulate(_rms(x_ref[...], g_ref[...]), sc_ref[...], sh_ref[...])
    h_ref[...] = h
    lg_ref[...] = _dot3(h, wr_ref[...], (((1,), (0,)), ((), ()))) + br_ref[...]


def _final_norm_kernel(x_ref, g_ref, o_ref):
    o_ref[...] = _rms(x_ref[...], g_ref[...])


class _Tok:
    def __init__(self, b, t, db, ts, d):
        assert ts == SUBLANES and t % NORM_TM == 0 and (db * ts) % NORM_TM == 0
        self.b, self.t, self.db, self.ts, self.d = b, t, db, ts, d
        self.ntp = b * t
        self.nt = b * t + db * ts
        self.rep = NORM_TM // SUBLANES
        self.n_ptiles = self.ntp // NORM_TM
        self.tiles_per_seq = t // NORM_TM
        self.n_tiles = self.nt // NORM_TM

    def rowblk(self, i):
        return jnp.where(i < self.n_ptiles, i // self.tiles_per_seq, self.b + i - self.n_ptiles)

    def mod_spec(self, layer, k, tn, grid_rank_fn):
        nj = self.d // tn

        def imap(*ids):
            i, j = grid_rank_fn(*ids)
            return (layer, self.rowblk(i), 0, k * nj + j)

        return pl.BlockSpec((None, self.rep, 1, tn), imap)


def _norm(tok, x, g, mod, layer, k_sc, k_sh, out_dtype):
    d = tok.d
    rank = lambda i: (i, 0)
    return pl.pallas_call(
        _norm_kernel,
        grid=(tok.n_tiles,),
        in_specs=[
            pl.BlockSpec((NORM_TM, d), lambda i: (i, 0)),
            _layer_vec_spec(layer, d),
            tok.mod_spec(layer, k_sc, d, rank),
            tok.mod_spec(layer, k_sh, d, rank),
        ],
        out_specs=pl.BlockSpec((NORM_TM, d), lambda i: (i, 0)),
        out_shape=jax.ShapeDtypeStruct((tok.nt, d), out_dtype),
        compiler_params=_cparams(("arbitrary",)),
        name="norm_mod",
    )(x, _layer_vec(g), mod, mod)


def _norm_router(tok, x, g, mod, layer, k_sc, k_sh, w_router, b_router):
    d = tok.d
    ne = w_router.shape[-1]
    rank = lambda i: (i, 0)
    return pl.pallas_call(
        _norm_router_kernel,
        grid=(tok.n_tiles,),
        in_specs=[
            pl.BlockSpec((NORM_TM, d), lambda i: (i, 0)),
            _layer_vec_spec(layer, d),
            tok.mod_spec(layer, k_sc, d, rank),
            tok.mod_spec(layer, k_sh, d, rank),
            pl.BlockSpec((None, d, ne), lambda i: (layer, 0, 0)),
            _layer_vec_spec(layer, ne),
        ],
        out_specs=[
            pl.BlockSpec((NORM_TM, d), lambda i: (i, 0)),
            pl.BlockSpec((NORM_TM, ne), lambda i: (i, 0)),
        ],
        out_shape=[
            jax.ShapeDtypeStruct((tok.nt, d), F32),
            jax.ShapeDtypeStruct((tok.nt, ne), F32),
        ],
        compiler_params=_cparams(("arbitrary",)),
        name="norm_mod_router",
    )(x, _layer_vec(g), mod, mod, w_router, _layer_vec(b_router))


def _final_norm(x, g):
    nt, d = x.shape
    return pl.pallas_call(
        _final_norm_kernel,
        grid=(nt // NORM_TM,),
        in_specs=[
            pl.BlockSpec((NORM_TM, d), lambda i: (i, 0)),
            pl.BlockSpec((1, d), lambda i: (0, 0)),
        ],
        out_specs=pl.BlockSpec((NORM_TM, d), lambda i: (i, 0)),
        out_shape=jax.ShapeDtypeStruct((nt, d), F32),
        compiler_params=_cparams(("arbitrary",)),
        name="final_norm",
    )(x, g.reshape(1, d))


def _mm_kernel(x_ref, w_ref, o_ref, wb_ref):
    @pl.when(pl.program_id(1) == 0)
    def _():
        wb_ref[...] = w_ref[...].astype(BF16)

    o_ref[...] = jnp.dot(x_ref[...], wb_ref[...], preferred_element_type=F32).astype(o_ref.dtype)


def _mm(x, w, layer, out_dtype):
    m, k = x.shape
    n = w.shape[-1]
    tm = _pick(m, (1024, 512, 256))
    tn = _pick(n, (1024, 512, 256, 128))
    return pl.pallas_call(
        _mm_kernel,
        grid=(n // tn, m // tm),
        in_specs=[
            pl.BlockSpec((tm, k), lambda j, i: (i, 0)),
            pl.BlockSpec((None, k, tn), lambda j, i: (layer, 0, j)),
        ],
        out_specs=pl.BlockSpec((tm, tn), lambda j, i: (i, j)),
        out_shape=jax.ShapeDtypeStruct((m, n), out_dtype),
        scratch_shapes=[pltpu.VMEM((k, tn), BF16)],
        compiler_params=_cparams(("arbitrary", "arbitrary")),
        name="proj_in",
    )(x, w)


def _mm_res_kernel(x_ref, w_ref, res_ref, gt_ref, o_ref, wb_ref):
    @pl.when(pl.program_id(1) == 0)
    def _():
        wb_ref[...] = w_ref[...].astype(BF16)

    y = jnp.dot(x_ref[...], wb_ref[...], preferred_element_type=F32)
    tm, tn = y.shape
    y3 = y.reshape(tm // SUBLANES, SUBLANES, tn) * gt_ref[...]
    o_ref[...] = res_ref[...] + y3.reshape(tm, tn)


def _mm_res(tok, x, w, layer, res, mod, k_gt):
    m, k = x.shape
    n = w.shape[-1]
    tn = _pick(n, (1024, 512, 256, 128))
    return pl.pallas_call(
        _mm_res_kernel,
        grid=(n // tn, tok.n_tiles),
        in_specs=[
            pl.BlockSpec((NORM_TM, k), lambda j, i: (i, 0)),
            pl.BlockSpec((None, k, tn), lambda j, i: (layer, 0, j)),
            pl.BlockSpec((NORM_TM, tn), lambda j, i: (i, j)),
            tok.mod_spec(layer, k_gt, tn, lambda j, i: (i, j)),
        ],
        out_specs=pl.BlockSpec((NORM_TM, tn), lambda j, i: (i, j)),
        out_shape=jax.ShapeDtypeStruct((m, n), F32),
        scratch_shapes=[pltpu.VMEM((k, tn), BF16)],
        compiler_params=_cparams(("arbitrary", "arbitrary")),
        name="proj_out_residual",
    )(x, w, res, mod)


def _hgrn_chunk(q, fr, iv, lb, st):
    c, dk = q.shape
    sig = jax.nn.sigmoid(fr)
    lf = jnp.log(lb + (1.0 - lb) * sig)
    kk = (1.0 - lb) * jax.nn.sigmoid(-fr)
    row = lax.broadcasted_iota(jnp.int32, (c, dk), 0)
    b = lf
    sh = 1
    while sh < c:
        b = b + jnp.where(row >= sh, pltpu.roll(b, sh, 0), 0.0)
        sh *= 2
    qe = q * jnp.exp(b)
    o_inter = lax.dot_general(qe.astype(BF16), st.astype(BF16), NT_DIMS, preferred_element_type=F32)
    ngrp = c // SUBLANES
    accs = [jnp.zeros((SUBLANES, iv.shape[1]), F32) for _ in range(ngrp)]
    for s in range(c):
        bs = b[s:s + 1, :]
        ks = kk[s:s + 1, :]
        vs = iv[s:s + 1, :]
        for gi in range(s // SUBLANES, ngrp):
            rs = slice(gi * SUBLANES, (gi + 1) * SUBLANES)
            p = q[rs] * jnp.exp(jnp.minimum(b[rs] - bs, 0.0)) * ks
            if gi == s // SUBLANES:
                p = jnp.where(row[rs] >= s, p, 0.0)
            accs[gi] = accs[gi] + jnp.sum(p, axis=-1, keepdims=True) * vs
    o = o_inter + jnp.concatenate(accs, axis=0)
    b_end = b[c - 1:c, :]
    kdec = kk * jnp.exp(b_end - b)
    upd = lax.dot_general(iv.astype(BF16), kdec.astype(BF16), TN_DIMS, preferred_element_type=F32)
    st_new = st * jnp.exp(b_end) + upd
    return o, st_new


def _hgrn_out(o, ga, gw):
    o = o * lax.rsqrt(jnp.mean(o * o, axis=-1, keepdims=True) + EPS) * gw
    return o * (ga * jax.nn.sigmoid(ga))


def _hgrn_prompt_kernel(q_ref, f_ref, i_ref, g_ref, lb_ref, gw_ref, o_ref, s_ref, st_ref, *, heads, dk):
    t = pl.program_id(1)
    tc = q_ref.shape[0]

    @pl.when(t == 0)
    def _():
        st_ref[...] = jnp.zeros_like(st_ref)

    def body(ci, carry):
        r0 = pl.multiple_of(ci * HGRN_CHUNK, HGRN_CHUNK)
        rows = pl.ds(r0, HGRN_CHUNK)
        for h in range(heads):
            cols = slice(h * dk, (h + 1) * dk)
            o, st_new = _hgrn_chunk(q_ref[rows, cols], f_ref[rows, cols], i_ref[rows, cols],
                                    lb_ref[:, cols], st_ref[h])
            st_ref[h] = st_new
            o_ref[rows, cols] = _hgrn_out(o, g_ref[rows, cols], gw_ref[:, cols]).astype(o_ref.dtype)
        return carry

    lax.fori_loop(0, tc // HGRN_CHUNK, body, 0)

    @pl.when(t == pl.num_programs(1) - 1)
    def _():
        for h in range(heads):
            s_ref[0, h] = st_ref[h].T


def _hgrn_prompt(proj, lb, gw, b, t, heads, dk, layer):
    wa = heads * dk
    tc = _pick(t, (256, 128, 64, 32, 16))
    nt = t // tc
    col = lambda k: pl.BlockSpec((tc, wa), lambda bi, ti: (bi * nt + ti, k))
    vec = _layer_vec_spec(layer, wa)
    return pl.pallas_call(
        functools.partial(_hgrn_prompt_kernel, heads=heads, dk=dk),
        grid=(b, nt),
        in_specs=[col(0), col(1), col(2), col(3), vec, vec],
        out_specs=[
            pl.BlockSpec((tc, wa), lambda bi, ti: (bi * nt + ti, 0)),
            pl.BlockSpec((1, heads, dk, dk), lambda bi, ti: (bi, 0, 0, 0)),
        ],
        out_shape=[
            jax.ShapeDtypeStruct((b * t, wa), BF16),
            jax.ShapeDtypeStruct((b, heads, dk, dk), F32),
        ],
        scratch_shapes=[pltpu.VMEM((heads, dk, dk), F32)],
        compiler_params=_cparams(("arbitrary", "arbitrary")),
        name="hgrn_prompt",
    )(proj, proj, proj, proj, _layer_vec(lb), _layer_vec(gw))


def _hgrn_sample_kernel(q_ref, f_ref, i_ref, g_ref, lb_ref, gw_ref, s0_ref, o_ref, s_ref, *, heads, dk, ts):
    sb = q_ref.shape[0] // ts
    for si in range(sb):
        rows = slice(si * ts, (si + 1) * ts)
        for h in range(heads):
            cols = slice(h * dk, (h + 1) * dk)
            o, st_new = _hgrn_chunk(q_ref[rows, cols], f_ref[rows, cols], i_ref[rows, cols],
                                    lb_ref[:, cols], s0_ref[si, h].T)
            s_ref[si, h] = st_new.T
            o_ref[rows, cols] = _hgrn_out(o, g_ref[rows, cols], gw_ref[:, cols]).astype(o_ref.dtype)


def _hgrn_sample(proj, lb, gw, s0, row0, db, ts, heads, dk, layer):
    wa = heads * dk
    sb = 2
    assert db % sb == 0 and row0 % (sb * ts) == 0
    blk0 = row0 // (sb * ts)
    col = lambda k: pl.BlockSpec((sb * ts, wa), lambda si: (blk0 + si, k))
    vec = _layer_vec_spec(layer, wa)
    return pl.pallas_call(
        functools.partial(_hgrn_sample_kernel, heads=heads, dk=dk, ts=ts),
        grid=(db // sb,),
        in_specs=[col(0), col(1), col(2), col(3), vec, vec,
                  pl.BlockSpec((None, sb, heads, dk, dk), lambda si: (layer, si, 0, 0, 0))],
        out_specs=[
            pl.BlockSpec((sb * ts, wa), lambda si: (si, 0)),
            pl.BlockSpec((sb, heads, dk, dk), lambda si: (si, 0, 0, 0)),
        ],
        out_shape=[
            jax.ShapeDtypeStruct((db * ts, wa), F32),
            jax.ShapeDtypeStruct((db, heads, dk, dk), F32),
        ],
        compiler_params=_cparams(("arbitrary",)),
        name="hgrn_sample",
    )(proj, proj, proj, proj, _layer_vec(lb), _layer_vec(gw), s0)


def _topk_select(gates, n_valid):
    nb = len(gates)
    sels = []
    for n in range(nb):
        cnt = jnp.zeros(gates[n].shape, F32)
        for m in range(nb):
            if m == n:
                continue
            ahead = (gates[m] >= gates[n]) if m < n else (gates[m] > gates[n])
            one = 1.0 if n_valid is None else jnp.where(m < n_valid, 1.0, 0.0)
            cnt = cnt + jnp.where(ahead, one, 0.0)
        one = 1.0 if n_valid is None else jnp.where(n < n_valid, 1.0, 0.0)
        sels.append(jnp.where(cnt < float(MOBA_TOPK), one, 0.0))
    return sels


def _moba_prompt_kernel(slope_ref, q_ref, k_ref, v_ref, o_ref, kmean_ref, sel_ref, *, nb, scale):
    h = pl.program_id(1)
    j = pl.program_id(2)
    bs = MOBA_BLOCK
    slope = slope_ref[h]

    @pl.when(j == 0)
    def _():
        rows = [jnp.sum(k_ref[n * bs:(n + 1) * bs, :], axis=0, keepdims=True) * (1.0 / bs) for n in range(nb)]
        kmean_ref[...] = jnp.concatenate(rows, axis=0)

    q = q_ref[...]
    gt = _dot3(kmean_ref[...], q, NT_DIMS)
    sels = _topk_select([gt[n:n + 1, :] for n in range(nb)], j)
    sel_ref[...] = jnp.concatenate(sels, axis=0)

    qs = (q * scale).astype(BF16)
    key_i = lax.broadcasted_iota(jnp.int32, (bs, bs), 0)
    qry_i = lax.broadcasted_iota(jnp.int32, (bs, bs), 1)
    dist0 = qry_i - key_i

    def scores(n):
        r0 = pl.multiple_of(n * bs, bs)
        kn = k_ref[pl.ds(r0, bs), :].astype(BF16)
        s = lax.dot_general(kn, qs, NT_DIMS, preferred_element_type=F32)
        return s - slope * (dist0 + (j - n) * bs).astype(F32), r0

    def pv(p, r0):
        vn = v_ref[pl.ds(r0, bs), :].astype(BF16)
        return lax.dot_general(vn, p.astype(BF16), TN_DIMS, preferred_element_type=F32)

    s, r0 = scores(j)
    s = jnp.where(dist0 >= 0, s, -jnp.inf)
    m = jnp.max(s, axis=0, keepdims=True)
    p = jnp.exp(s - m)
    l = jnp.sum(p, axis=0, keepdims=True)
    acc = pv(p, r0)

    def body(n, carry):
        m, l, acc = carry
        s, r0 = scores(n)
        s = jnp.where(sel_ref[pl.ds(n, 1), :] > 0.0, s, -jnp.inf)
        m_new = jnp.maximum(m, jnp.max(s, axis=0, keepdims=True))
        alpha = jnp.exp(m - m_new)
        p = jnp.exp(s - m_new)
        l = alpha * l + jnp.sum(p, axis=0, keepdims=True)
        acc = alpha * acc + pv(p, r0)
        return m_new, l, acc

    m, l, acc = lax.fori_loop(0, j, body, (m, l, acc))
    o_ref[...] = (acc / l).T.astype(o_ref.dtype)


def _moba_prompt(proj, slopes, b, t, heads, dh, q_col, k_col, v_col):
    assert t % MOBA_BLOCK == 0
    nb = t // MOBA_BLOCK
    return pl.pallas_call(
        functools.partial(_moba_prompt_kernel, nb=nb, scale=dh ** -0.5),
        grid=(b, heads, nb),
        in_specs=[
            pl.BlockSpec(memory_space=pltpu.SMEM),
            pl.BlockSpec((MOBA_BLOCK, dh), lambda bi, h, j: (bi * nb + j, q_col + h)),
            pl.BlockSpec((t, dh), lambda bi, h, j: (bi, k_col + h)),
            pl.BlockSpec((t, dh), lambda bi, h, j: (bi, v_col + h)),
        ],
        out_specs=pl.BlockSpec((MOBA_BLOCK, dh), lambda bi, h, j: (bi * nb + j, h)),
        out_shape=jax.ShapeDtypeStruct((b * t, heads * dh), BF16),
        scratch_shapes=[pltpu.VMEM((nb, dh), F32), pltpu.VMEM((nb, MOBA_BLOCK), F32)],
        compiler_params=_cparams(("arbitrary", "arbitrary", "arbitrary")),
        name="moba_prompt",
    )(slopes, proj, proj, proj)


def _moba_sample_kernel(pt_ref, q_ref, kn_ref, vn_ref, ck_hbm, cv_hbm, o_ref, kbuf, vbuf, s_scr, sem,
                        *, layer, n_pages, page, heads, dh, ts, scale):
    hq = heads * ts
    pages_per_blk = MOBA_BLOCK // page
    nb = n_pages // pages_per_blk
    past_len = n_pages * page
    seq = pl.program_id(0)
    n_seq = pl.num_programs(0)
    slot = seq % 2

    def page_copies(s_idx, slot_idx):
        cps = []
        for p in range(n_pages):
            pg = pt_ref[s_idx, p]
            cps.append(pltpu.make_async_copy(ck_hbm.at[layer, pg], kbuf.at[slot_idx, p], sem.at[slot_idx, 0]))
            cps.append(pltpu.make_async_copy(cv_hbm.at[layer, pg], vbuf.at[slot_idx, p], sem.at[slot_idx, 1]))
        return cps

    @pl.when(seq == 0)
    def _():
        for cp in page_copies(seq, slot):
            cp.start()

    @pl.when(seq + 1 < n_seq)
    def _():
        for cp in page_copies(seq + 1, 1 - slot):
            cp.start()

    for cp in page_copies(seq, slot):
        cp.wait()
    k_refs = [kbuf.at[slot, p] for p in range(n_pages)]
    v_refs = [vbuf.at[slot, p] for p in range(n_pages)]

    q = q_ref[...]
    qall = jnp.concatenate([q[:, h * dh:(h + 1) * dh] for h in range(heads)], axis=0)
    qs = (qall * scale).astype(BF16)
    lane = lax.broadcasted_iota(jnp.int32, (heads, hq), 1)
    sub = lax.broadcasted_iota(jnp.int32, (heads, hq), 0)
    own_head = (lane // ts) == sub
    tq = (lane % ts).astype(F32)
    slope = jnp.exp2(-(sub + 1).astype(F32) * (8.0 / heads))

    gates = []
    for n in range(nb):
        ksum = jnp.zeros((heads, dh), F32)
        for pp in range(pages_per_blk):
            ksum = ksum + jnp.sum(k_refs[n * pages_per_blk + pp][...], axis=0)
        gates.append(_dot3(ksum * (1.0 / MOBA_BLOCK), qall, NT_DIMS))
    sels = _topk_select(gates, None)

    m = jnp.full((heads, hq), -jnp.inf, F32)
    r_loc = lax.broadcasted_iota(jnp.int32, (page, heads, hq), 0).astype(F32)
    for p in range(n_pages):
        k2 = k_refs[p][...].reshape(page * heads, dh).astype(BF16)
        s = lax.dot_general(k2, qs, NT_DIMS, preferred_element_type=F32).reshape(page, heads, hq)
        dist = (past_len - p * page) + tq[None] - r_loc
        s = s - slope[None] * dist
        s = jnp.where(sels[p // pages_per_blk][None] > 0.0, s, -jnp.inf)
        s_scr[p * page:(p + 1) * page] = s
        m = jnp.maximum(m, jnp.max(s, axis=0))
    kn2 = kn_ref[0].reshape(ts * heads, dh).astype(BF16)
    sn = lax.dot_general(kn2, qs, NT_DIMS, preferred_element_type=F32).reshape(ts, heads, hq)
    tk = lax.broadcasted_iota(jnp.int32, (ts, heads, hq), 0).astype(F32)
    sn = sn - slope[None] * (tq[None] - tk)
    sn = jnp.where(tk <= tq[None], sn, -jnp.inf)
    m = jnp.maximum(m, jnp.max(sn, axis=0))

    acc = jnp.zeros((hq, dh), F32)
    den = jnp.zeros((hq, dh), F32)

    def accumulate(acc, den, s, v2):
        pm = jnp.where(own_head[None], jnp.exp(s - m[None]), 0.0)
        pm2 = pm.reshape(s.shape[0] * heads, hq).astype(BF16)
        acc = acc + lax.dot_general(pm2, v2, TN_DIMS, preferred_element_type=F32)
        den = den + lax.dot_general(pm2, jnp.ones(v2.shape, BF16), TN_DIMS, preferred_element_type=F32)
        return acc, den

    for p in range(n_pages):
        v2 = v_refs[p][...].reshape(page * heads, dh).astype(BF16)
        acc, den = accumulate(acc, den, s_scr[p * page:(p + 1) * page], v2)
    acc, den = accumulate(acc, den, sn, vn_ref[0].reshape(ts * heads, dh).astype(BF16))
    o = acc / den
    for h in range(heads):
        o_ref[:, h * dh:(h + 1) * dh] = o[h * ts:(h + 1) * ts, :].astype(o_ref.dtype)


def _moba_sample(proj, kn4, vn4, cache_k, cache_v, page_table, layer, row0, q_col):
    db, ts, heads, dh = kn4.shape
    n_pages = page_table.shape[1]
    page = cache_k.shape[2]
    assert MOBA_BLOCK % page == 0 and (n_pages * page) % MOBA_BLOCK == 0 and ts <= MOBA_BLOCK
    assert row0 % ts == 0
    wb = heads * dh

    grid_spec = pltpu.PrefetchScalarGridSpec(
        num_scalar_prefetch=1,
        grid=(db,),
        in_specs=[
            pl.BlockSpec((ts, wb), lambda s, pt: (row0 // ts + s, q_col)),
            pl.BlockSpec((1, ts, heads, dh), lambda s, pt: (s, 0, 0, 0)),
            pl.BlockSpec((1, ts, heads, dh), lambda s, pt: (s, 0, 0, 0)),
            pl.BlockSpec(memory_space=pl.ANY),
            pl.BlockSpec(memory_space=pl.ANY),
        ],
        out_specs=pl.BlockSpec((ts, wb), lambda s, pt: (s, 0)),
        scratch_shapes=[
            pltpu.VMEM((2, n_pages, page, heads, dh), F32),
            pltpu.VMEM((2, n_pages, page, heads, dh), F32),
            pltpu.VMEM((n_pages * page, heads, heads * ts), F32),
            pltpu.SemaphoreType.DMA((2, 2)),
        ],
    )
    return pl.pallas_call(
        functools.partial(_moba_sample_kernel, layer=layer, n_pages=n_pages, page=page, heads=heads, dh=dh,
                          ts=ts, scale=dh ** -0.5),
        grid_spec=grid_spec,
        out_shape=jax.ShapeDtypeStruct((db * ts, wb), F32),
        compiler_params=_cparams(("arbitrary",)),
        name="moba_sample",
    )(page_table, proj, kn4, vn4, cache_k, cache_v)


def _conv_prompt_kernel(b_ref, c_ref, h_ref, w_ref, o_ref, st_ref, ext_ref):
    t = pl.program_id(1)
    tc = b_ref.shape[0]
    pad = SUBLANES

    @pl.when(t == 0)
    def _():
        ext_ref[0:pad, :] = jnp.zeros((pad, ext_ref.shape[1]), F32)

    @pl.when(t > 0)
    def _():
        ext_ref[0:pad, :] = ext_ref[tc:tc + pad, :]

    ext_ref[pad:pad + tc, :] = c_ref[...] * h_ref[...]
    y = (w_ref[0:1, :] * ext_ref[pad - 2:pad - 2 + tc, :]
         + w_ref[1:2, :] * ext_ref[pad - 1:pad - 1 + tc, :]
         + w_ref[2:3, :] * ext_ref[pad:pad + tc, :])
    o_ref[...] = (b_ref[...] * y).astype(o_ref.dtype)

    @pl.when(t == pl.num_programs(1) - 1)
    def _():
        st_ref[0] = ext_ref[pad + tc - (CONV_W - 1):pad + tc, :]


def _conv_prompt(proj, conv_w, b, t, wc, layer, col0):
    tc = _pick(t, (256, 128, 64, 32, 16, 8))
    nt = t // tc
    col = lambda k: pl.BlockSpec((tc, wc), lambda bi, ti: (bi * nt + ti, col0 + k))
    return pl.pallas_call(
        _conv_prompt_kernel,
        grid=(b, nt),
        in_specs=[col(0), col(1), col(2),
                  pl.BlockSpec((None, CONV_W, wc), lambda bi, ti: (layer, 0, 0))],
        out_specs=[
            pl.BlockSpec((tc, wc), lambda bi, ti: (bi * nt + ti, 0)),
            pl.BlockSpec((1, CONV_W - 1, wc), lambda bi, ti: (bi, 0, 0)),
        ],
        out_shape=[
            jax.ShapeDtypeStruct((b * t, wc), BF16),
            jax.ShapeDtypeStruct((b, CONV_W - 1, wc), F32),
        ],
        scratch_shapes=[pltpu.VMEM((tc + 2 * SUBLANES, wc), F32)],
        compiler_params=_cparams(("arbitrary", "arbitrary")),
        name="conv_prompt",
    )(proj, proj, proj, conv_w)


def _conv_sample_kernel(b_ref, c_ref, h_ref, w_ref, buf_ref, o_ref, st_ref, ext_ref, *, ts):
    rows, wc = b_ref.shape
    ns = rows // ts
    pad = SUBLANES
    u = (c_ref[...] * h_ref[...]).reshape(ns, ts, wc)
    ext_ref[:, pad - (CONV_W - 1):pad, :] = buf_ref[...]
    ext_ref[:, pad:pad + ts, :] = u
    w = w_ref[...]
    y = (w[0:1, :][None] * ext_ref[:, pad - 2:pad - 2 + ts, :]
         + w[1:2, :][None] * ext_ref[:, pad - 1:pad - 1 + ts, :]
         + w[2:3, :][None] * u)
    o_ref[...] = (b_ref[...] * y.reshape(rows, wc)).astype(o_ref.dtype)
    st_ref[...] = ext_ref[:, pad + ts - (CONV_W - 1):pad + ts, :]


def _conv_sample(proj, conv_w, buf, row0, db, ts, wc, layer, col0):
    sb = _pick(db, (32, 16, 8, 4, 2, 1))
    assert row0 % (sb * ts) == 0 and ts >= CONV_W - 1
    blk0 = row0 // (sb * ts)
    col = lambda k: pl.BlockSpec((sb * ts, wc), lambda si: (blk0 + si, col0 + k))
    return pl.pallas_call(
        functools.partial(_conv_sample_kernel, ts=ts),
        grid=(db // sb,),
        in_specs=[col(0), col(1), col(2),
                  pl.BlockSpec((None, CONV_W, wc), lambda si: (layer, 0, 0)),
                  pl.BlockSpec((None, sb, CONV_W - 1, wc), lambda si: (layer, si, 0, 0))],
        out_specs=[
            pl.BlockSpec((sb * ts, wc), lambda si: (si, 0)),
            pl.BlockSpec((sb, CONV_W - 1, wc), lambda si: (si, 0, 0)),
        ],
        out_shape=[
            jax.ShapeDtypeStruct((db * ts, wc), BF16),
            jax.ShapeDtypeStruct((db, CONV_W - 1, wc), F32),
        ],
        scratch_shapes=[pltpu.VMEM((sb, SUBLANES + ts, wc), F32)],
        compiler_params=_cparams(("arbitrary",)),
        name="conv_sample",
    )(proj, proj, proj, conv_w, buf)


def _merge_kernel(oa_ref, ob_ref, oc_ref, ga_ref, gb_ref, gc_ref, wa_ref, wb_ref, wc_ref, o_ref,
                  sa_ref, sb_ref, sc_ref):
    @pl.when(pl.program_id(1) == 0)
    def _():
        sa_ref[...] = wa_ref[...].astype(BF16)
        sb_ref[...] = wb_ref[...].astype(BF16)
        sc_ref[...] = wc_ref[...].astype(BF16)

    def branch(o_r, g_r, w_r):
        y = jnp.dot(o_r[...].astype(BF16), w_r[...], preferred_element_type=F32)
        return jax.nn.sigmoid(g_r[...]) * y

    acc = branch(oa_ref, ga_ref, sa_ref) + branch(ob_ref, gb_ref, sb_ref) + branch(oc_ref, gc_ref, sc_ref)
    o_ref[...] = acc.astype(o_ref.dtype)


def _merge(o_a, o_b, o_c, proj, gate_col0, w_a, w_b, w_c, layer, d):
    m = o_a.shape[0]
    tm = _pick(m, (512, 256))
    tn = _pick(d, (512, 256, 128))
    assert gate_col0 % tn == 0
    g0 = gate_col0 // tn
    nj = d // tn
    act = lambda a: pl.BlockSpec((tm, a.shape[1]), lambda j, i: (i, 0))
    gate = lambda k: pl.BlockSpec((tm, tn), lambda j, i: (i, g0 + k * nj + j))
    wgt = lambda w: pl.BlockSpec((None, w.shape[1], tn), lambda j, i: (layer, 0, j))
    return pl.pallas_call(
        _merge_kernel,
        grid=(nj, m // tm),
        in_specs=[act(o_a), act(o_b), act(o_c), gate(0), gate(1), gate(2), wgt(w_a), wgt(w_b), wgt(w_c)],
        out_specs=pl.BlockSpec((tm, tn), lambda j, i: (i, j)),
        out_shape=jax.ShapeDtypeStruct((m, d), BF16),
        scratch_shapes=[pltpu.VMEM((w.shape[1], tn), BF16) for w in (w_a, w_b, w_c)],
        compiler_params=_cparams(("arbitrary", "arbitrary")),
        name="branch_merge",
    )(o_a, o_b, o_c, proj, proj, proj, w_a, w_b, w_c)


def _moe_kernel(ge_ref, gr0_ref, gnb_ref, rtok_ref, rdst_ref, h_hbm, wg_ref, wu_ref, bg_ref, bu_ref,
                wd_ref, bd_ref, y_hbm, xf_ref, xb_ref, acc_ref, sem_in, sem_out):
    del ge_ref
    g = pl.program_id(0)
    c = pl.program_id(1)
    nb = gnb_ref[g]
    r0 = gr0_ref[g]
    nrows = nb * MOE_BLOCK
    max_blocks = xf_ref.shape[0] // MOE_BLOCK

    def gather_copy(r, tok):
        return pltpu.make_async_copy(h_hbm.at[pl.ds(tok, 1)], xf_ref.at[pl.ds(r, 1)], sem_in)

    def scatter_copy(r, dst):
        return pltpu.make_async_copy(acc_ref.at[pl.ds(r, 1)], y_hbm.at[pl.ds(dst, 1)], sem_out)

    @pl.when((c == 0) & (nb > 0))
    def _():
        def issue(r, carry):
            gather_copy(r, rtok_ref[r0 + r]).start()
            return carry

        lax.fori_loop(0, nrows, issue, 0)

        def zero(bi, carry):
            xf_ref[pl.ds(pl.multiple_of(bi * MOE_BLOCK, MOE_BLOCK), MOE_BLOCK), :] = jnp.zeros(
                (MOE_BLOCK, xf_ref.shape[1]), F32)
            return carry

        lax.fori_loop(nb, max_blocks, zero, 0)

        def wait(r, carry):
            gather_copy(r, 0).wait()
            return carry

        lax.fori_loop(0, nrows, wait, 0)
        xb_ref[...] = xf_ref[...].astype(BF16)
        acc_ref[...] = jnp.zeros_like(acc_ref)

    @pl.when(nb > 0)
    def _():
        x = xb_ref[...]
        gate = jnp.dot(x, wg_ref[...].astype(BF16), preferred_element_type=F32) + bg_ref[...]
        lin = jnp.dot(x, wu_ref[...].astype(BF16), preferred_element_type=F32) + bu_ref[...]
        gate = jnp.minimum(gate, SWIGLU_LIMIT)
        lin = jnp.clip(lin, -SWIGLU_LIMIT, SWIGLU_LIMIT)
        act = gate * jax.nn.sigmoid(SWIGLU_ALPHA * gate) * (lin + 1.0)
        acc_ref[...] += jnp.dot(act.astype(BF16), wd_ref[...].astype(BF16), preferred_element_type=F32)

    @pl.when((c == pl.num_programs(1) - 1) & (nb > 0))
    def _():
        acc_ref[...] += bd_ref[...]

        def issue(r, carry):
            dst = rdst_ref[r0 + r]

            @pl.when(dst >= 0)
            def _():
                scatter_copy(r, dst).start()

            return carry

        lax.fori_loop(0, nrows, issue, 0)

        def wait(r, carry):
            @pl.when(rdst_ref[r0 + r] >= 0)
            def _():
                scatter_copy(r, 0).wait()

            return carry

        lax.fori_loop(0, nrows, wait, 0)


def _moe(h2, meta, w_gu, b_gu, w_dn, b_dn, layer, n_slots):
    grp_e, grp_r0, grp_nb, row_tok, row_dst = meta
    nt, d = h2.shape
    d_ff = w_dn.shape[2]
    ne = w_gu.shape[1]
    tf = _pick(d_ff, (MOE_FF_TILE, 128))
    nc = d_ff // tf
    ng = grp_e.shape[0]
    rows = MOE_GROUP_BLOCKS * MOE_BLOCK

    def chunk(g, c, gnb):
        return jnp.where(gnb[g] > 0, c, nc - 1)

    grid_spec = pltpu.PrefetchScalarGridSpec(
        num_scalar_prefetch=5,
        grid=(ng, nc),
        in_specs=[
            pl.BlockSpec(memory_space=pl.ANY),
            pl.BlockSpec((None, None, d, tf), lambda g, c, ge, gr, gnb, rt, rd: (layer, ge[g], 0, chunk(g, c, gnb))),
            pl.BlockSpec((None, None, d, tf), lambda g, c, ge, gr, gnb, rt, rd: (layer, ge[g], 0, nc + chunk(g, c, gnb))),
            pl.BlockSpec((None, None, 1, tf), lambda g, c, ge, gr, gnb, rt, rd: (layer, ge[g], 0, chunk(g, c, gnb))),
            pl.BlockSpec((None, None, 1, tf), lambda g, c, ge, gr, gnb, rt, rd: (layer, ge[g], 0, nc + chunk(g, c, gnb))),
            pl.BlockSpec((None, None, tf, d), lambda g, c, ge, gr, gnb, rt, rd: (layer, ge[g], chunk(g, c, gnb), 0)),
            pl.BlockSpec((None, None, 1, d), lambda g, c, ge, gr, gnb, rt, rd: (layer, ge[g], 0, 0)),
        ],
        out_specs=pl.BlockSpec(memory_space=pl.ANY),
        scratch_shapes=[
            pltpu.VMEM((rows, d), F32),
            pltpu.VMEM((rows, d), BF16),
            pltpu.VMEM((rows, d), F32),
            pltpu.SemaphoreType.DMA(()),
            pltpu.SemaphoreType.DMA(()),
        ],
    )
    depth = w_gu.shape[0]
    return pl.pallas_call(
        _moe_kernel,
        grid_spec=grid_spec,
        out_shape=jax.ShapeDtypeStruct((n_slots * nt, d), F32),
        compiler_params=_cparams(("arbitrary", "arbitrary")),
        name="moe_experts",
    )(grp_e, grp_r0, grp_nb, row_tok, row_dst, h2, w_gu, w_gu,
      b_gu.reshape(depth, ne, 1, 2 * d_ff), b_gu.reshape(depth, ne, 1, 2 * d_ff),
      w_dn, b_dn.reshape(depth, ne, 1, d))


def _moe_routing(logits, n_experts):
    n = logits.shape[0]
    top_v, top_e = lax.top_k(logits, TOP_K)
    top_w = jax.nn.softmax(top_v, axis=-1)
    e_flat = top_e.reshape(-1).astype(jnp.int32)
    order = jnp.argsort(e_flat).astype(jnp.int32)
    e_s = e_flat[order]
    t_s = order // TOP_K
    slot_s = order % TOP_K
    counts = jnp.zeros((n_experts,), jnp.int32).at[e_flat].add(1)
    start = jnp.cumsum(counts) - counts
    nblk = (counts + MOE_BLOCK - 1) // MOE_BLOCK
    p_end = jnp.cumsum(nblk * MOE_BLOCK)
    p_start = p_end - nblk * MOE_BLOCK
    dest = p_start[e_s] + jnp.arange(n * TOP_K, dtype=jnp.int32) - start[e_s]
    n_blk_max = -(-(n * TOP_K) // MOE_BLOCK) + n_experts
    rows = n_blk_max * MOE_BLOCK + MOE_GROUP_BLOCKS * MOE_BLOCK
    row_tok = jnp.full((rows,), n - 1, jnp.int32).at[dest].set(t_s)
    row_dst = jnp.full((rows,), -1, jnp.int32).at[dest].set(slot_s * n + t_s)
    gb = MOE_GROUP_BLOCKS
    ngrp_e = (nblk + gb - 1) // gb
    g_end = jnp.cumsum(ngrp_e)
    n_groups = n_blk_max // gb + n_experts
    gid = jnp.arange(n_groups, dtype=jnp.int32)
    valid = gid < g_end[-1]
    ge = jnp.minimum(jnp.searchsorted(g_end, gid, side='right'), n_experts - 1).astype(jnp.int32)
    last_e = jnp.minimum(jnp.searchsorted(g_end, g_end[-1] - 1, side='right'), n_experts - 1).astype(jnp.int32)
    ge = jnp.where(valid, ge, last_e)
    k_in = gid - (g_end[ge] - ngrp_e[ge])
    gr0 = jnp.where(valid, p_start[ge] + k_in * gb * MOE_BLOCK, 0).astype(jnp.int32)
    gnb = jnp.where(valid, jnp.minimum(gb, nblk[ge] - k_in * gb), 0).astype(jnp.int32)
    return top_w, (ge, gr0, gnb, row_tok, row_dst)


def _combine_kernel(x_ref, w_ref, gt_ref, *refs):
    y_refs, o_ref = refs[:-1], refs[-1]
    w = w_ref[...]
    acc = w[:, 0:1] * y_refs[0][...]
    for k in range(1, len(y_refs)):
        acc = acc + w[:, k:k + 1] * y_refs[k][...]
    tm, d = acc.shape
    y3 = acc.reshape(tm // SUBLANES, SUBLANES, d) * gt_ref[...]
    o_ref[...] = x_ref[...] + y3.reshape(tm, d)


def _combine(tok, x, top_w, y_slots, mod, layer, k_gt):
    d = tok.d
    rank = lambda i: (i, 0)
    slot = lambda k: pl.BlockSpec((NORM_TM, d), lambda i: (k * tok.n_tiles + i, 0))
    return pl.pallas_call(
        _combine_kernel,
        grid=(tok.n_tiles,),
        in_specs=[
            pl.BlockSpec((NORM_TM, d), lambda i: (i, 0)),
            pl.BlockSpec((NORM_TM, TOP_K), lambda i: (i, 0)),
            tok.mod_spec(layer, k_gt, d, rank),
        ] + [slot(k) for k in range(TOP_K)],
        out_specs=pl.BlockSpec((NORM_TM, d), lambda i: (i, 0)),
        out_shape=jax.ShapeDtypeStruct((tok.nt, d), F32),
        compiler_params=_cparams(("arbitrary",)),
        name="moe_combine",
    )(x, top_w, mod, *([y_slots] * TOP_K))


def kernel(x_prompt, x_sample, cache_k, cache_v, state_hgrn, state_conv, page_table, c_prompt, c_sample,
           w_ada, b_ada, g_norm_mix, w_in, hgrn_lb_logits, g_hgrn_out, conv_w, w_br_a, w_br_b, w_br_c, w_o,
           g_norm_ffn, w_router, b_router, w_gu, b_gu, w_dn, b_dn, g_final):
    b, t, d = x_prompt.shape
    db, ts, _ = x_sample.shape
    depth = w_ada.shape[0]
    heads_a, dk_a = state_hgrn.shape[2], state_hgrn.shape[3]
    heads_b, dh_b = cache_k.shape[3], cache_k.shape[4]
    w_a, w_b, w_c = heads_a * dk_a, heads_b * dh_b, state_conv.shape[3]
    n_experts = w_router.shape[-1]
    tok = _Tok(b, t, db, ts, d)
    ntp = tok.ntp

    off_b = 4 * w_a
    off_c = off_b + 3 * w_b
    off_g = off_c + 3 * w_c
    assert w_in.shape[-1] == off_g + 3 * d

    x = jnp.concatenate([x_prompt.reshape(ntp, d), x_sample.reshape(db * ts, d)], axis=0)
    c_rows = jnp.concatenate([jnp.repeat(c_prompt, tok.rep, axis=0), c_sample], axis=0)
    mod = _ada(c_rows, w_ada, b_ada)
    mod = mod.reshape(depth, mod.shape[1], 1, 6 * d)

    lb_soft = jax.nn.softmax(hgrn_lb_logits.astype(F32), axis=0)
    lower_bounds = jnp.cumsum(lb_soft, axis=0) - lb_soft[0:1]
    slopes = jnp.asarray(np.exp2((np.arange(heads_b, dtype=np.float32) + 1.0) * (-8.0 / heads_b)), F32)

    ks_p, vs_p, ks_s, vs_s, ss_p, ss_s, bs_p, bs_s = [], [], [], [], [], [], [], []
    for l in range(depth):
        h = _norm(tok, x, g_norm_mix, mod, l, 1, 0, BF16)
        proj = _mm(h, w_in, l, F32)

        k_new = proj[:, off_b + w_b:off_b + 2 * w_b]
        v_new = proj[:, off_b + 2 * w_b:off_b + 3 * w_b]
        kp4 = k_new[:ntp].reshape(b, t, heads_b, dh_b)
        vp4 = v_new[:ntp].reshape(b, t, heads_b, dh_b)
        ks4 = k_new[ntp:].reshape(db, ts, heads_b, dh_b)
        vs4 = v_new[ntp:].reshape(db, ts, heads_b, dh_b)

        oa_p, s_p = _hgrn_prompt(proj, lower_bounds, g_hgrn_out, b, t, heads_a, dk_a, l)
        oa_s, s_s = _hgrn_sample(proj, lower_bounds, g_hgrn_out, state_hgrn, ntp, db, ts, heads_a, dk_a, l)
        ob_p = _moba_prompt(proj, slopes, b, t, heads_b, dh_b,
                            off_b // dh_b, (off_b + w_b) // dh_b, (off_b + 2 * w_b) // dh_b)
        ob_s = _moba_sample(proj, ks4, vs4, cache_k, cache_v, page_table, l, ntp, off_b // w_b)
        oc_p, cb_p = _conv_prompt(proj, conv_w, b, t, w_c, l, off_c // w_c)
        oc_s, cb_s = _conv_sample(proj, conv_w, state_conv, ntp, db, ts, w_c, l, off_c // w_c)

        o_a = jnp.concatenate([oa_p, oa_s.astype(BF16)], axis=0)
        o_b = jnp.concatenate([ob_p, ob_s.astype(BF16)], axis=0)
        o_c = jnp.concatenate([oc_p, oc_s], axis=0)
        merged = _merge(o_a, o_b, o_c, proj, off_g, w_br_a, w_br_b, w_br_c, l, d)
        x = _mm_res(tok, merged, w_o, l, x, mod, 2)

        h2, logits = _norm_router(tok, x, g_norm_ffn, mod, l, 4, 3, w_router, b_router)
        top_w, meta = _moe_routing(logits, n_experts)
        y_slots = _moe(h2, meta, w_gu, b_gu, w_dn, b_dn, l, TOP_K)
        x = _combine(tok, x, top_w, y_slots, mod, l, 5)

        ks_p.append(kp4)
        vs_p.append(vp4)
        ks_s.append(ks4)
        vs_s.append(vs4)
        ss_p.append(s_p)
        ss_s.append(s_s)
        bs_p.append(cb_p)
        bs_s.append(cb_s)

    y = _final_norm(x, g_final)
    return (y[:ntp].reshape(b, t, d), y[ntp:].reshape(db, ts, d),
            jnp.stack(ks_p), jnp.stack(vs_p), jnp.stack(ks_s), jnp.stack(vs_s),
            jnp.stack(ss_p), jnp.stack(ss_s), jnp.stack(bs_p), jnp.stack(bs_s))
```

```python
import functools
import math

import jax
import jax.numpy as jnp
import numpy as np
from jax import lax
from jax.experimental import pallas as pl
from jax.experimental.pallas import tpu as pltpu

F32 = jnp.float32
BF16 = jnp.bfloat16

SUBLANES = 8
LANES = 128

EPS = 1e-6
MOBA_BLOCK = 256
MOBA_TOPK = 3
TOP_K = 4
SWIGLU_LIMIT = 7.0
SWIGLU_ALPHA = 1.702
MOE_BLOCK = 128
CONV_W = 3

HGRN_CHUNK = 16
NORM_TM = 256
MOE_GROUP_BLOCKS = 10
MOE_FF_TILE = 256
VMEM_LIMIT = 56 * 1024 * 1024

NT_DIMS = (((1,), (1,)), ((), ()))
TN_DIMS = (((0,), (0,)), ((), ()))


def _cparams(sem):
    return pltpu.CompilerParams(dimension_semantics=sem, vmem_limit_bytes=VMEM_LIMIT)


def _pick(n, cands):
    for c in cands:
        if n % c == 0:
            return c
    raise ValueError(f"no tile in {cands} divides {n}")


def _layer_vec(v):
    return v.reshape(v.shape[0], 1, v.shape[1])


def _layer_vec_spec(layer, n):
    return pl.BlockSpec((None, 1, n), lambda *_: (layer, 0, 0))


def _split_bf16(x):
    hi = x.astype(BF16)
    lo = (x - hi.astype(F32)).astype(BF16)
    return hi, lo


def _dot3(a, b, dims):
    ah, al = _split_bf16(a)
    bh, bl = _split_bf16(b)
    dg = functools.partial(lax.dot_general, dimension_numbers=dims, preferred_element_type=F32)
    return dg(ah, bh) + dg(al, bh) + dg(ah, bl)


def _ada_kernel(c_ref, w_ref, b_ref, o_ref):
    c = c_ref[...]
    ca = (c * jax.nn.sigmoid(c)).astype(BF16)
    o_ref[0] = jnp.dot(ca, w_ref[0].astype(BF16), preferred_element_type=F32) + b_ref[0]


def _ada(c_rows, w_ada, b_ada):
    depth, d, n6 = w_ada.shape
    r = c_rows.shape[0]
    tn = _pick(n6, (1024, 512, 256, 128))
    return pl.pallas_call(
        _ada_kernel,
        grid=(depth, n6 // tn),
        in_specs=[
            pl.BlockSpec((r, d), lambda l, j: (0, 0)),
            pl.BlockSpec((1, d, tn), lambda l, j: (l, 0, j)),
            pl.BlockSpec((1, 1, tn), lambda l, j: (l, 0, j)),
        ],
        out_specs=pl.BlockSpec((1, r, tn), lambda l, j: (l, 0, j)),
        out_shape=jax.ShapeDtypeStruct((depth, r, n6), F32),
        compiler_params=_cparams(("arbitrary", "arbitrary")),
        name="ada_mod",
    )(c_rows, w_ada, b_ada.reshape(depth, 1, n6))


def _rms(x, g):
    ms = jnp.mean(x * x, axis=-1, keepdims=True)
    return x * lax.rsqrt(ms + EPS) * g


def _modulate(y, sc, sh):
    tm, d = y.shape
    y3 = y.reshape(tm // SUBLANES, SUBLANES, d)
    return (y3 * (1.0 + sc) + sh).reshape(tm, d)


def _norm_kernel(x_ref, g_ref, sc_ref, sh_ref, o_ref):
    h = _modulate(_rms(x_ref[...], g_ref[...]), sc_ref[...], sh_ref[...])
    o_ref[...] = h.astype(o_ref.dtype)


def _norm_router_kernel(x_ref, g_ref, sc_ref, sh_ref, wr_ref, br_ref, h_ref, lg_ref):
    h = _modulate(_rms(x_ref[...], g_ref[...]), sc_ref[...], sh_ref[...])
    tm, d = h.shape
    nl = d // LANES
    for j in range(nl):
        h_ref[pl.ds(j, tm, stride=nl), :] = h[:, j * LANES:(j + 1) * LANES]
    lg_ref[...] = _dot3(h, wr_ref[...], (((1,), (0,)), ((), ()))) + br_ref[...]


def _final_norm_kernel(x_ref, g_ref, o_ref):
    o_ref[...] = _rms(x_ref[...], g_ref[...])


class _Tok:
    def __init__(self, b, t, db, ts, d):
        assert ts == SUBLANES and t % NORM_TM == 0 and (db * ts) % NORM_TM == 0
        self.b, self.t, self.db, self.ts, self.d = b, t, db, ts, d
        self.ntp = b * t
        self.nt = b * t + db * ts
        self.rep = NORM_TM // SUBLANES
        self.n_ptiles = self.ntp // NORM_TM
        self.tiles_per_seq = t // NORM_TM
        self.n_tiles = self.nt // NORM_TM

    def rowblk(self, i):
        return jnp.where(i < self.n_ptiles, i // self.tiles_per_seq, self.b + i - self.n_ptiles)

    def mod_spec(self, layer, k, tn, grid_rank_fn):
        nj = self.d // tn

        def imap(*ids):
            i, j = grid_rank_fn(*ids)
            return (layer, self.rowblk(i), 0, k * nj + j)

        return pl.BlockSpec((None, self.rep, 1, tn), imap)


def _norm(tok, x, g, mod, layer, k_sc, k_sh, out_dtype):
    d = tok.d
    rank = lambda i: (i, 0)
    return pl.pallas_call(
        _norm_kernel,
        grid=(tok.n_tiles,),
        in_specs=[
            pl.BlockSpec((NORM_TM, d), lambda i: (i, 0)),
            _layer_vec_spec(layer, d),
            tok.mod_spec(layer, k_sc, d, rank),
            tok.mod_spec(layer, k_sh, d, rank),
        ],
        out_specs=pl.BlockSpec((NORM_TM, d), lambda i: (i, 0)),
        out_shape=jax.ShapeDtypeStruct((tok.nt, d), out_dtype),
        compiler_params=_cparams(("arbitrary",)),
        name="norm_mod",
    )(x, _layer_vec(g), mod, mod)


def _norm_router(tok, x, g, mod, layer, k_sc, k_sh, w_router, b_router):
    d = tok.d
    ne = w_router.shape[-1]
    rank = lambda i: (i, 0)
    return pl.pallas_call(
        _norm_router_kernel,
        grid=(tok.n_tiles,),
        in_specs=[
            pl.BlockSpec((NORM_TM, d), lambda i: (i, 0)),
            _layer_vec_spec(layer, d),
            tok.mod_spec(layer, k_sc, d, rank),
            tok.mod_spec(layer, k_sh, d, rank),
            pl.BlockSpec((None, d, ne), lambda i: (layer, 0, 0)),
            _layer_vec_spec(layer, ne),
        ],
        out_specs=[
            pl.BlockSpec((NORM_TM * (d // LANES), LANES), lambda i: (i, 0)),
            pl.BlockSpec((NORM_TM, ne), lambda i: (i, 0)),
        ],
        out_shape=[
            jax.ShapeDtypeStruct((tok.nt * (d // LANES), LANES), F32),
            jax.ShapeDtypeStruct((tok.nt, ne), F32),
        ],
        compiler_params=_cparams(("arbitrary",)),
        name="norm_mod_router",
    )(x, _layer_vec(g), mod, mod, w_router, _layer_vec(b_router))


def _final_norm(x, g):
    nt, d = x.shape
    return pl.pallas_call(
        _final_norm_kernel,
        grid=(nt // NORM_TM,),
        in_specs=[
            pl.BlockSpec((NORM_TM, d), lambda i: (i, 0)),
            pl.BlockSpec((1, d), lambda i: (0, 0)),
        ],
        out_specs=pl.BlockSpec((NORM_TM, d), lambda i: (i, 0)),
        out_shape=jax.ShapeDtypeStruct((nt, d), F32),
        compiler_params=_cparams(("arbitrary",)),
        name="final_norm",
    )(x, g.reshape(1, d))


def _mm_kernel(x_ref, w_ref, o_ref, wb_ref):
    @pl.when(pl.program_id(1) == 0)
    def _():
        wb_ref[...] = w_ref[...].astype(BF16)

    o_ref[...] = jnp.dot(x_ref[...], wb_ref[...], preferred_element_type=F32).astype(o_ref.dtype)


def _mm(x, w, layer, out_dtype):
    m, k = x.shape
    n = w.shape[-1]
    tm = _pick(m, (1024, 512, 256))
    tn = _pick(n, (1024, 512, 256, 128))
    return pl.pallas_call(
        _mm_kernel,
        grid=(n // tn, m // tm),
        in_specs=[
            pl.BlockSpec((tm, k), lambda j, i: (i, 0)),
            pl.BlockSpec((None, k, tn), lambda j, i: (layer, 0, j)),
        ],
        out_specs=pl.BlockSpec((tm, tn), lambda j, i: (i, j)),
        out_shape=jax.ShapeDtypeStruct((m, n), out_dtype),
        scratch_shapes=[pltpu.VMEM((k, tn), BF16)],
        compiler_params=_cparams(("arbitrary", "arbitrary")),
        name="proj_in",
    )(x, w)


def _mm_res_kernel(x_ref, w_ref, res_ref, gt_ref, o_ref, wb_ref):
    @pl.when(pl.program_id(1) == 0)
    def _():
        wb_ref[...] = w_ref[...].astype(BF16)

    y = jnp.dot(x_ref[...], wb_ref[...], preferred_element_type=F32)
    tm, tn = y.shape
    y3 = y.reshape(tm // SUBLANES, SUBLANES, tn) * gt_ref[...]
    o_ref[...] = res_ref[...] + y3.reshape(tm, tn)


def _mm_res(tok, x, w, layer, res, mod, k_gt):
    m, k = x.shape
    n = w.shape[-1]
    tn = _pick(n, (1024, 512, 256, 128))
    return pl.pallas_call(
        _mm_res_kernel,
        grid=(n // tn, tok.n_tiles),
        in_specs=[
            pl.BlockSpec((NORM_TM, k), lambda j, i: (i, 0)),
            pl.BlockSpec((None, k, tn), lambda j, i: (layer, 0, j)),
            pl.BlockSpec((NORM_TM, tn), lambda j, i: (i, j)),
            tok.mod_spec(layer, k_gt, tn, lambda j, i: (i, j)),
        ],
        out_specs=pl.BlockSpec((NORM_TM, tn), lambda j, i: (i, j)),
        out_shape=jax.ShapeDtypeStruct((m, n), F32),
        scratch_shapes=[pltpu.VMEM((k, tn), BF16)],
        compiler_params=_cparams(("arbitrary", "arbitrary")),
        name="proj_out_residual",
    )(x, w, res, mod)


def _hgrn_chunk(q, fr, iv, lb, st):
    c, dk = q.shape
    sig = jax.nn.sigmoid(fr)
    lf = jnp.log(lb + (1.0 - lb) * sig)
    kk = (1.0 - lb) * jax.nn.sigmoid(-fr)
    row = lax.broadcasted_iota(jnp.int32, (c, dk), 0)
    b = lf
    sh = 1
    while sh < c:
        b = b + jnp.where(row >= sh, pltpu.roll(b, sh, 0), 0.0)
        sh *= 2
    qe = q * jnp.exp(b)
    o_inter = lax.dot_general(qe.astype(BF16), st.astype(BF16), NT_DIMS, preferred_element_type=F32)
    ngrp = c // SUBLANES
    accs = [jnp.zeros((SUBLANES, iv.shape[1]), F32) for _ in range(ngrp)]
    for s in range(c):
        bs = b[s:s + 1, :]
        ks = kk[s:s + 1, :]
        vs = iv[s:s + 1, :]
        for gi in range(s // SUBLANES, ngrp):
            rs = slice(gi * SUBLANES, (gi + 1) * SUBLANES)
            p = q[rs] * jnp.exp(jnp.minimum(b[rs] - bs, 0.0)) * ks
            if gi == s // SUBLANES:
                p = jnp.where(row[rs] >= s, p, 0.0)
            accs[gi] = accs[gi] + jnp.sum(p, axis=-1, keepdims=True) * vs
    o = o_inter + jnp.concatenate(accs, axis=0)
    b_end = b[c - 1:c, :]
    kdec = kk * jnp.exp(b_end - b)
    upd = lax.dot_general(iv.astype(BF16), kdec.astype(BF16), TN_DIMS, preferred_element_type=F32)
    st_new = st * jnp.exp(b_end) + upd
    return o, st_new


def _hgrn_out(o, ga, gw):
    o = o * lax.rsqrt(jnp.mean(o * o, axis=-1, keepdims=True) + EPS) * gw
    return o * (ga * jax.nn.sigmoid(ga))


def _hgrn_prompt_kernel(q_ref, f_ref, i_ref, g_ref, lb_ref, gw_ref, o_all_ref, o_ref, s_ref, st_ref,
                        *, heads, dk):
    del o_all_ref
    t = pl.program_id(1)
    tc = q_ref.shape[0]

    @pl.when(t == 0)
    def _():
        st_ref[...] = jnp.zeros_like(st_ref)

    def body(ci, carry):
        r0 = pl.multiple_of(ci * HGRN_CHUNK, HGRN_CHUNK)
        rows = pl.ds(r0, HGRN_CHUNK)
        for h in range(heads):
            cols = slice(h * dk, (h + 1) * dk)
            o, st_new = _hgrn_chunk(q_ref[rows, cols], f_ref[rows, cols], i_ref[rows, cols],
                                    lb_ref[:, cols], st_ref[h])
            st_ref[h] = st_new
            o_ref[rows, cols] = _hgrn_out(o, g_ref[rows, cols], gw_ref[:, cols]).astype(o_ref.dtype)
        return carry

    lax.fori_loop(0, tc // HGRN_CHUNK, body, 0)

    @pl.when(t == pl.num_programs(1) - 1)
    def _():
        for h in range(heads):
            s_ref[0, h] = st_ref[h].T


def _hgrn_prompt(proj, lb, gw, o_all, b, t, heads, dk, layer):
    wa = heads * dk
    tc = _pick(t, (256, 128, 64, 32, 16))
    nt = t // tc
    col = lambda k: pl.BlockSpec((tc, wa), lambda bi, ti: (bi * nt + ti, k))
    vec = _layer_vec_spec(layer, wa)
    return pl.pallas_call(
        functools.partial(_hgrn_prompt_kernel, heads=heads, dk=dk),
        grid=(b, nt),
        in_specs=[col(0), col(1), col(2), col(3), vec, vec, pl.BlockSpec(memory_space=pl.ANY)],
        out_specs=[
            pl.BlockSpec((tc, wa), lambda bi, ti: (bi * nt + ti, 0)),
            pl.BlockSpec((1, heads, dk, dk), lambda bi, ti: (bi, 0, 0, 0)),
        ],
        out_shape=[
            jax.ShapeDtypeStruct(o_all.shape, o_all.dtype),
            jax.ShapeDtypeStruct((b, heads, dk, dk), F32),
        ],
        input_output_aliases={6: 0},
        scratch_shapes=[pltpu.VMEM((heads, dk, dk), F32)],
        compiler_params=_cparams(("arbitrary", "arbitrary")),
        name="hgrn_prompt",
    )(proj, proj, proj, proj, _layer_vec(lb), _layer_vec(gw), o_all)


def _hgrn_sample_kernel(q_ref, f_ref, i_ref, g_ref, lb_ref, gw_ref, s0_ref, o_all_ref, o_ref, s_ref,
                        *, heads, dk, ts):
    del o_all_ref
    sb = q_ref.shape[0] // ts
    for si in range(sb):
        rows = slice(si * ts, (si + 1) * ts)
        for h in range(heads):
            cols = slice(h * dk, (h + 1) * dk)
            o, st_new = _hgrn_chunk(q_ref[rows, cols], f_ref[rows, cols], i_ref[rows, cols],
                                    lb_ref[:, cols], s0_ref[si, h].T)
            s_ref[si, h] = st_new.T
            o_ref[rows, cols] = _hgrn_out(o, g_ref[rows, cols], gw_ref[:, cols]).astype(o_ref.dtype)


def _hgrn_sample(proj, lb, gw, s0, o_all, row0, db, ts, heads, dk, layer):
    wa = heads * dk
    sb = 2
    assert db % sb == 0 and row0 % (sb * ts) == 0
    blk0 = row0 // (sb * ts)
    col = lambda k: pl.BlockSpec((sb * ts, wa), lambda si: (blk0 + si, k))
    vec = _layer_vec_spec(layer, wa)
    return pl.pallas_call(
        functools.partial(_hgrn_sample_kernel, heads=heads, dk=dk, ts=ts),
        grid=(db // sb,),
        in_specs=[col(0), col(1), col(2), col(3), vec, vec,
                  pl.BlockSpec((None, sb, heads, dk, dk), lambda si: (layer, si, 0, 0, 0)),
                  pl.BlockSpec(memory_space=pl.ANY)],
        out_specs=[
            pl.BlockSpec((sb * ts, wa), lambda si: (blk0 + si, 0)),
            pl.BlockSpec((sb, heads, dk, dk), lambda si: (si, 0, 0, 0)),
        ],
        out_shape=[
            jax.ShapeDtypeStruct(o_all.shape, o_all.dtype),
            jax.ShapeDtypeStruct((db, heads, dk, dk), F32),
        ],
        input_output_aliases={7: 0},
        compiler_params=_cparams(("arbitrary",)),
        name="hgrn_sample",
    )(proj, proj, proj, proj, _layer_vec(lb), _layer_vec(gw), s0, o_all)


def _topk_select(gates, n_valid):
    nb = len(gates)
    sels = []
    for n in range(nb):
        cnt = jnp.zeros(gates[n].shape, F32)
        for m in range(nb):
            if m == n:
                continue
            ahead = (gates[m] >= gates[n]) if m < n else (gates[m] > gates[n])
            one = 1.0 if n_valid is None else jnp.where(m < n_valid, 1.0, 0.0)
            cnt = cnt + jnp.where(ahead, one, 0.0)
        one = 1.0 if n_valid is None else jnp.where(n < n_valid, 1.0, 0.0)
        sels.append(jnp.where(cnt < float(MOBA_TOPK), one, 0.0))
    return sels


def _moba_prompt_kernel(slope_ref, q_ref, k_ref, v_ref, o_all_ref, o_ref, s_scr, *, nb, scale):
    del o_all_ref
    bs = MOBA_BLOCK
    slope = slope_ref[pl.program_id(1)]
    blk = lambda ref, n: ref[n * bs:(n + 1) * bs, :]
    kmean = jnp.concatenate(
        [jnp.sum(blk(k_ref, n), axis=0, keepdims=True) * (1.0 / bs) for n in range(nb)], axis=0)
    key_i = lax.broadcasted_iota(jnp.int32, (bs, bs), 0)
    qry_i = lax.broadcasted_iota(jnp.int32, (bs, bs), 1)
    dist0 = (qry_i - key_i).astype(F32)
    causal = dist0 >= 0.0
    bias0 = slope * dist0

    for j in range(nb):
        q = blk(q_ref, j)
        qs = (q * scale).astype(BF16)
        sels = None
        if j > MOBA_TOPK:
            gt = _dot3(kmean[0:j, :], q, NT_DIMS)
            sels = _topk_select([gt[n:n + 1, :] for n in range(j)], None)
        m = None
        for n in range(j + 1):
            s = lax.dot_general(blk(k_ref, n).astype(BF16), qs, NT_DIMS, preferred_element_type=F32)
            s = s - (bias0 + slope * float((j - n) * bs))
            if n == j:
                s = jnp.where(causal, s, -jnp.inf)
            elif sels is not None:
                s = jnp.where(sels[n] > 0.0, s, -jnp.inf)
            s_scr[n * bs:(n + 1) * bs, :] = s
            mn = jnp.max(s, axis=0, keepdims=True)
            m = mn if m is None else jnp.maximum(m, mn)
        l = jnp.zeros((1, bs), F32)
        acc = jnp.zeros((q.shape[1], bs), F32)
        for n in range(j + 1):
            p = jnp.exp(blk(s_scr, n) - m)
            l = l + jnp.sum(p, axis=0, keepdims=True)
            acc = acc + lax.dot_general(blk(v_ref, n).astype(BF16), p.astype(BF16), TN_DIMS,
                                        preferred_element_type=F32)
        o_ref[j * bs:(j + 1) * bs, :] = (acc / l).T.astype(o_ref.dtype)


def _moba_prompt(proj, slopes, o_all, b, t, heads, dh, q_col, k_col, v_col):
    assert t % MOBA_BLOCK == 0
    nb = t // MOBA_BLOCK
    seq_blk = lambda col: pl.BlockSpec((t, dh), lambda bi, h: (bi, col + h))
    return pl.pallas_call(
        functools.partial(_moba_prompt_kernel, nb=nb, scale=dh ** -0.5),
        grid=(b, heads),
        in_specs=[pl.BlockSpec(memory_space=pltpu.SMEM), seq_blk(q_col), seq_blk(k_col), seq_blk(v_col),
                  pl.BlockSpec(memory_space=pl.ANY)],
        out_specs=pl.BlockSpec((t, dh), lambda bi, h: (bi, h)),
        out_shape=jax.ShapeDtypeStruct(o_all.shape, o_all.dtype),
        input_output_aliases={4: 0},
        scratch_shapes=[pltpu.VMEM((t, MOBA_BLOCK), F32)],
        compiler_params=_cparams(("arbitrary", "arbitrary")),
        name="moba_prompt",
    )(slopes, proj, proj, proj, o_all)


def _moba_sample_kernel(pt_ref, q_ref, kn_ref, vn_ref, ck_hbm, cv_hbm, o_all_ref, o_ref, kbuf, vbuf, s_scr, sem,
                        *, layer, n_pages, page, heads, dh, ts, scale):
    del o_all_ref
    hq = heads * ts
    pages_per_blk = MOBA_BLOCK // page
    nb = n_pages // pages_per_blk
    past_len = n_pages * page
    seq = pl.program_id(0)
    n_seq = pl.num_programs(0)
    slot = seq % 2

    def page_copies(s_idx, slot_idx):
        cps = []
        for p in range(n_pages):
            pg = pt_ref[s_idx, p]
            cps.append(pltpu.make_async_copy(ck_hbm.at[layer, pg], kbuf.at[slot_idx, p], sem.at[slot_idx, 0]))
            cps.append(pltpu.make_async_copy(cv_hbm.at[layer, pg], vbuf.at[slot_idx, p], sem.at[slot_idx, 1]))
        return cps

    @pl.when(seq == 0)
    def _():
        for cp in page_copies(seq, slot):
            cp.start()

    @pl.when(seq + 1 < n_seq)
    def _():
        for cp in page_copies(seq + 1, 1 - slot):
            cp.start()

    for cp in page_copies(seq, slot):
        cp.wait()
    k_refs = [kbuf.at[slot, p] for p in range(n_pages)]
    v_refs = [vbuf.at[slot, p] for p in range(n_pages)]

    q = q_ref[...]
    qall = jnp.concatenate([q[:, h * dh:(h + 1) * dh] for h in range(heads)], axis=0)
    qs = (qall * scale).astype(BF16)
    lane = lax.broadcasted_iota(jnp.int32, (heads, hq), 1)
    sub = lax.broadcasted_iota(jnp.int32, (heads, hq), 0)
    own_head = (lane // ts) == sub
    tq = (lane % ts).astype(F32)
    slope = jnp.exp2(-(sub + 1).astype(F32) * (8.0 / heads))

    gates = []
    for n in range(nb):
        ksum = jnp.zeros((heads, dh), F32)
        for pp in range(pages_per_blk):
            ksum = ksum + jnp.sum(k_refs[n * pages_per_blk + pp][...], axis=0)
        gates.append(_dot3(ksum * (1.0 / MOBA_BLOCK), qall, NT_DIMS))
    sels = _topk_select(gates, None)

    m = jnp.full((heads, hq), -jnp.inf, F32)
    r_loc = lax.broadcasted_iota(jnp.int32, (page, heads, hq), 0).astype(F32)
    for p in range(n_pages):
        k2 = k_refs[p][...].reshape(page * heads, dh).astype(BF16)
        s = lax.dot_general(k2, qs, NT_DIMS, preferred_element_type=F32).reshape(page, heads, hq)
        dist = (past_len - p * page) + tq[None] - r_loc
        s = s - slope[None] * dist
        s = jnp.where(sels[p // pages_per_blk][None] > 0.0, s, -jnp.inf)
        s_scr[p * page:(p + 1) * page] = s
        m = jnp.maximum(m, jnp.max(s, axis=0))
    kn2 = kn_ref[0].reshape(ts * heads, dh).astype(BF16)
    sn = lax.dot_general(kn2, qs, NT_DIMS, preferred_element_type=F32).reshape(ts, heads, hq)
    tk = lax.broadcasted_iota(jnp.int32, (ts, heads, hq), 0).astype(F32)
    sn = sn - slope[None] * (tq[None] - tk)
    sn = jnp.where(tk <= tq[None], sn, -jnp.inf)
    m = jnp.maximum(m, jnp.max(sn, axis=0))

    acc = jnp.zeros((hq, dh), F32)
    den = jnp.zeros((hq, dh), F32)

    def accumulate(acc, den, s, v2):
        pm = jnp.where(own_head[None], jnp.exp(s - m[None]), 0.0)
        pm2 = pm.reshape(s.shape[0] * heads, hq).astype(BF16)
        acc = acc + lax.dot_general(pm2, v2, TN_DIMS, preferred_element_type=F32)
        den = den + lax.dot_general(pm2, jnp.ones(v2.shape, BF16), TN_DIMS, preferred_element_type=F32)
        return acc, den

    for p in range(n_pages):
        v2 = v_refs[p][...].reshape(page * heads, dh).astype(BF16)
        acc, den = accumulate(acc, den, s_scr[p * page:(p + 1) * page], v2)
    acc, den = accumulate(acc, den, sn, vn_ref[0].reshape(ts * heads, dh).astype(BF16))
    o = acc / den
    for h in range(heads):
        o_ref[:, h * dh:(h + 1) * dh] = o[h * ts:(h + 1) * ts, :].astype(o_ref.dtype)


def _moba_sample(proj, kn4, vn4, cache_k, cache_v, page_table, o_all, layer, row0, q_col):
    db, ts, heads, dh = kn4.shape
    n_pages = page_table.shape[1]
    page = cache_k.shape[2]
    assert MOBA_BLOCK % page == 0 and (n_pages * page) % MOBA_BLOCK == 0 and ts <= MOBA_BLOCK
    assert row0 % ts == 0
    wb = heads * dh

    grid_spec = pltpu.PrefetchScalarGridSpec(
        num_scalar_prefetch=1,
        grid=(db,),
        in_specs=[
            pl.BlockSpec((ts, wb), lambda s, pt: (row0 // ts + s, q_col)),
            pl.BlockSpec((1, ts, heads, dh), lambda s, pt: (s, 0, 0, 0)),
            pl.BlockSpec((1, ts, heads, dh), lambda s, pt: (s, 0, 0, 0)),
            pl.BlockSpec(memory_space=pl.ANY),
            pl.BlockSpec(memory_space=pl.ANY),
            pl.BlockSpec(memory_space=pl.ANY),
        ],
        out_specs=pl.BlockSpec((ts, wb), lambda s, pt: (row0 // ts + s, 0)),
        scratch_shapes=[
            pltpu.VMEM((2, n_pages, page, heads, dh), F32),
            pltpu.VMEM((2, n_pages, page, heads, dh), F32),
            pltpu.VMEM((n_pages * page, heads, heads * ts), F32),
            pltpu.SemaphoreType.DMA((2, 2)),
        ],
    )
    return pl.pallas_call(
        functools.partial(_moba_sample_kernel, layer=layer, n_pages=n_pages, page=page, heads=heads, dh=dh,
                          ts=ts, scale=dh ** -0.5),
        grid_spec=grid_spec,
        out_shape=jax.ShapeDtypeStruct(o_all.shape, o_all.dtype),
        input_output_aliases={6: 0},
        compiler_params=_cparams(("arbitrary",)),
        name="moba_sample",
    )(page_table, proj, kn4, vn4, cache_k, cache_v, o_all)


def _conv_prompt_kernel(b_ref, c_ref, h_ref, w_ref, o_all_ref, o_ref, st_ref, ext_ref):
    del o_all_ref
    t = pl.program_id(1)
    tc = b_ref.shape[0]
    pad = SUBLANES

    @pl.when(t == 0)
    def _():
        ext_ref[0:pad, :] = jnp.zeros((pad, ext_ref.shape[1]), F32)

    @pl.when(t > 0)
    def _():
        ext_ref[0:pad, :] = ext_ref[tc:tc + pad, :]

    ext_ref[pad:pad + tc, :] = c_ref[...] * h_ref[...]
    y = (w_ref[0:1, :] * ext_ref[pad - 2:pad - 2 + tc, :]
         + w_ref[1:2, :] * ext_ref[pad - 1:pad - 1 + tc, :]
         + w_ref[2:3, :] * ext_ref[pad:pad + tc, :])
    o_ref[...] = (b_ref[...] * y).astype(o_ref.dtype)

    @pl.when(t == pl.num_programs(1) - 1)
    def _():
        st_ref[0] = ext_ref[pad + tc - (CONV_W - 1):pad + tc, :]


def _conv_prompt(proj, conv_w, o_all, b, t, wc, layer, col0):
    tc = _pick(t, (256, 128, 64, 32, 16))
    nt = t // tc
    col = lambda k: pl.BlockSpec((tc, wc), lambda bi, ti: (bi * nt + ti, col0 + k))
    return pl.pallas_call(
        _conv_prompt_kernel,
        grid=(b, nt),
        in_specs=[col(0), col(1), col(2),
                  pl.BlockSpec((None, CONV_W, wc), lambda bi, ti: (layer, 0, 0)),
                  pl.BlockSpec(memory_space=pl.ANY)],
        out_specs=[
            pl.BlockSpec((tc, wc), lambda bi, ti: (bi * nt + ti, 0)),
            pl.BlockSpec((1, CONV_W - 1, wc), lambda bi, ti: (bi, 0, 0)),
        ],
        out_shape=[
            jax.ShapeDtypeStruct(o_all.shape, o_all.dtype),
            jax.ShapeDtypeStruct((b, CONV_W - 1, wc), F32),
        ],
        input_output_aliases={4: 0},
        scratch_shapes=[pltpu.VMEM((tc + 2 * SUBLANES, wc), F32)],
        compiler_params=_cparams(("arbitrary", "arbitrary")),
        name="conv_prompt",
    )(proj, proj, proj, conv_w, o_all)


def _conv_sample_kernel(b_ref, c_ref, h_ref, w_ref, buf_ref, o_all_ref, o_ref, st_ref, ext_ref, *, ts):
    del o_all_ref
    rows, wc = b_ref.shape
    ns = rows // ts
    pad = SUBLANES
    u = (c_ref[...] * h_ref[...]).reshape(ns, ts, wc)
    ext_ref[:, pad - (CONV_W - 1):pad, :] = buf_ref[...]
    ext_ref[:, pad:pad + ts, :] = u
    w = w_ref[...]
    y = (w[0:1, :][None] * ext_ref[:, pad - 2:pad - 2 + ts, :]
         + w[1:2, :][None] * ext_ref[:, pad - 1:pad - 1 + ts, :]
         + w[2:3, :][None] * u)
    o_ref[...] = (b_ref[...] * y.reshape(rows, wc)).astype(o_ref.dtype)
    st_ref[...] = ext_ref[:, pad + ts - (CONV_W - 1):pad + ts, :]


def _conv_sample(proj, conv_w, buf, o_all, row0, db, ts, wc, layer, col0):
    sb = _pick(db, (32, 16, 8, 4, 2))
    assert row0 % (sb * ts) == 0 and ts >= CONV_W - 1
    blk0 = row0 // (sb * ts)
    col = lambda k: pl.BlockSpec((sb * ts, wc), lambda si: (blk0 + si, col0 + k))
    return pl.pallas_call(
        functools.partial(_conv_sample_kernel, ts=ts),
        grid=(db // sb,),
        in_specs=[col(0), col(1), col(2),
                  pl.BlockSpec((None, CONV_W, wc), lambda si: (layer, 0, 0)),
                  pl.BlockSpec((None, sb, CONV_W - 1, wc), lambda si: (layer, si, 0, 0)),
                  pl.BlockSpec(memory_space=pl.ANY)],
        out_specs=[
            pl.BlockSpec((sb * ts, wc), lambda si: (blk0 + si, 0)),
            pl.BlockSpec((sb, CONV_W - 1, wc), lambda si: (si, 0, 0)),
        ],
        out_shape=[
            jax.ShapeDtypeStruct(o_all.shape, o_all.dtype),
            jax.ShapeDtypeStruct((db, CONV_W - 1, wc), F32),
        ],
        input_output_aliases={5: 0},
        scratch_shapes=[pltpu.VMEM((sb, SUBLANES + ts, wc), F32)],
        compiler_params=_cparams(("arbitrary",)),
        name="conv_sample",
    )(proj, proj, proj, conv_w, buf, o_all)


def _merge_kernel(oa_ref, ob_ref, oc_ref, ga_ref, gb_ref, gc_ref, wa_ref, wb_ref, wc_ref, o_ref,
                  sa_ref, sb_ref, sc_ref):
    @pl.when(pl.program_id(1) == 0)
    def _():
        sa_ref[...] = wa_ref[...].astype(BF16)
        sb_ref[...] = wb_ref[...].astype(BF16)
        sc_ref[...] = wc_ref[...].astype(BF16)

    def branch(o_r, g_r, w_r):
        y = jnp.dot(o_r[...].astype(BF16), w_r[...], preferred_element_type=F32)
        return jax.nn.sigmoid(g_r[...]) * y

    acc = branch(oa_ref, ga_ref, sa_ref) + branch(ob_ref, gb_ref, sb_ref) + branch(oc_ref, gc_ref, sc_ref)
    o_ref[...] = acc.astype(o_ref.dtype)


def _merge(o_a, o_b, o_c, proj, gate_col0, w_a, w_b, w_c, layer, d):
    m = o_a.shape[0]
    tm = _pick(m, (512, 256))
    tn = _pick(d, (512, 256, 128))
    assert gate_col0 % tn == 0
    g0 = gate_col0 // tn
    nj = d // tn
    act = lambda a: pl.BlockSpec((tm, a.shape[1]), lambda j, i: (i, 0))
    gate = lambda k: pl.BlockSpec((tm, tn), lambda j, i: (i, g0 + k * nj + j))
    wgt = lambda w: pl.BlockSpec((None, w.shape[1], tn), lambda j, i: (layer, 0, j))
    return pl.pallas_call(
        _merge_kernel,
        grid=(nj, m // tm),
        in_specs=[act(o_a), act(o_b), act(o_c), gate(0), gate(1), gate(2), wgt(w_a), wgt(w_b), wgt(w_c)],
        out_specs=pl.BlockSpec((tm, tn), lambda j, i: (i, j)),
        out_shape=jax.ShapeDtypeStruct((m, d), BF16),
        scratch_shapes=[pltpu.VMEM((w.shape[1], tn), BF16) for w in (w_a, w_b, w_c)],
        compiler_params=_cparams(("arbitrary", "arbitrary")),
        name="branch_merge",
    )(o_a, o_b, o_c, proj, proj, proj, w_a, w_b, w_c)


def _moe_kernel(ge_ref, gr0_ref, gnb_ref, rtok_ref, rdst_ref, h_hbm, wg_ref, wu_ref, bg_ref, bu_ref,
                wd_ref, bd_ref, y_hbm, xf_ref, xb_ref, acc_ref, yo_ref, sem_in, sem_out):
    del ge_ref
    g = pl.program_id(0)
    c = pl.program_id(1)
    n_g = pl.num_programs(0)
    n_c = pl.num_programs(1)
    nb = gnb_ref[g]
    nl = xb_ref.shape[1] // LANES
    max_blocks = xb_ref.shape[0] // MOE_BLOCK
    unroll = 8
    g_next = jnp.minimum(g + 1, n_g - 1)
    has_next = (g + 1 < n_g) & (gnb_ref[g_next] > 0)

    def gather_copy(r, tok):
        return pltpu.make_async_copy(h_hbm.at[pl.ds(tok * nl, nl)], xf_ref.at[pl.ds(r * nl, nl)], sem_in)

    def scatter_copy(r, dst):
        return pltpu.make_async_copy(yo_ref.at[pl.ds(r * nl, nl)], y_hbm.at[pl.ds(dst * nl, nl)], sem_out)

    def issue_gather(grp):
        base = gr0_ref[grp]

        def body(i, carry):
            for u in range(unroll):
                r = i * unroll + u
                gather_copy(r, rtok_ref[base + r]).start()
            return carry

        lax.fori_loop(0, gnb_ref[grp] * (MOE_BLOCK // unroll), body, 0)

    def wait_gather(grp):
        def body(i, carry):
            for u in range(unroll):
                gather_copy(i * unroll + u, 0).wait()
            return carry

        lax.fori_loop(0, gnb_ref[grp] * (MOE_BLOCK // unroll), body, 0)

    def scatter_rows(grp, start):
        base = gr0_ref[grp]

        def body(i, carry):
            for u in range(unroll):
                r = i * unroll + u
                dst = rdst_ref[base + r]

                @pl.when(dst >= 0)
                def _():
                    cp = scatter_copy(r, jnp.maximum(dst, 0))
                    cp.start() if start else cp.wait()

            return carry

        lax.fori_loop(0, gnb_ref[grp] * (MOE_BLOCK // unroll), body, 0)

    @pl.when((c == 0) & (nb > 0))
    def _():
        @pl.when(g == 0)
        def _():
            issue_gather(g)

        wait_gather(g)

        def to_matmul_layout(bi, carry):
            row0 = pl.multiple_of(bi * MOE_BLOCK, MOE_BLOCK)
            for j in range(nl):
                xb_ref[pl.ds(row0, MOE_BLOCK), j * LANES:(j + 1) * LANES] = xf_ref[
                    pl.ds(row0 * nl + j, MOE_BLOCK, stride=nl), :].astype(BF16)
            return carry

        lax.fori_loop(0, nb, to_matmul_layout, 0)

        def zero(bi, carry):
            xb_ref[pl.ds(pl.multiple_of(bi * MOE_BLOCK, MOE_BLOCK), MOE_BLOCK), :] = jnp.zeros(
                (MOE_BLOCK, xb_ref.shape[1]), BF16)
            return carry

        lax.fori_loop(nb, max_blocks, zero, 0)
        acc_ref[...] = jnp.zeros_like(acc_ref)

        @pl.when(has_next)
        def _():
            issue_gather(g_next)

    @pl.when(nb > 0)
    def _():
        x = xb_ref[...]
        gate = jnp.dot(x, wg_ref[...].astype(BF16), preferred_element_type=F32) + bg_ref[...]
        lin = jnp.dot(x, wu_ref[...].astype(BF16), preferred_element_type=F32) + bu_ref[...]
        gate = jnp.minimum(gate, SWIGLU_LIMIT)
        lin = jnp.clip(lin, -SWIGLU_LIMIT, SWIGLU_LIMIT)
        act = gate * jax.nn.sigmoid(SWIGLU_ALPHA * gate) * (lin + 1.0)
        acc_ref[...] += jnp.dot(act.astype(BF16), wd_ref[...].astype(BF16), preferred_element_type=F32)

    @pl.when((c == n_c - 1) & (nb > 0))
    def _():
        @pl.when(g > 0)
        def _():
            scatter_rows(g - 1, start=False)

        def to_token_layout(bi, carry):
            row0 = pl.multiple_of(bi * MOE_BLOCK, MOE_BLOCK)
            for j in range(nl):
                cols = slice(j * LANES, (j + 1) * LANES)
                yo_ref[pl.ds(row0 * nl + j, MOE_BLOCK, stride=nl), :] = (
                    acc_ref[pl.ds(row0, MOE_BLOCK), cols] + bd_ref[:, cols])
            return carry

        lax.fori_loop(0, nb, to_token_layout, 0)
        scatter_rows(g, start=True)

        @pl.when(jnp.logical_not(has_next))
        def _():
            scatter_rows(g, start=False)


def _moe(h2, meta, w_gu, b_gu, w_dn, b_dn, layer, n_slots):
    grp_e, grp_r0, grp_nb, row_tok, row_dst = meta
    d = w_dn.shape[3]
    nl = d // LANES
    nt = h2.shape[0] // nl
    d_ff = w_dn.shape[2]
    ne = w_gu.shape[1]
    tf = _pick(d_ff, (MOE_FF_TILE, 128))
    nc = d_ff // tf
    ng = grp_e.shape[0]
    rows = MOE_GROUP_BLOCKS * MOE_BLOCK

    def chunk(g, c, gnb):
        return jnp.where(gnb[g] > 0, c, nc - 1)

    grid_spec = pltpu.PrefetchScalarGridSpec(
        num_scalar_prefetch=5,
        grid=(ng, nc),
        in_specs=[
            pl.BlockSpec(memory_space=pl.ANY),
            pl.BlockSpec((None, None, d, tf), lambda g, c, ge, gr, gnb, rt, rd: (layer, ge[g], 0, chunk(g, c, gnb))),
            pl.BlockSpec((None, None, d, tf), lambda g, c, ge, gr, gnb, rt, rd: (layer, ge[g], 0, nc + chunk(g, c, gnb))),
            pl.BlockSpec((None, None, 1, tf), lambda g, c, ge, gr, gnb, rt, rd: (layer, ge[g], 0, chunk(g, c, gnb))),
            pl.BlockSpec((None, None, 1, tf), lambda g, c, ge, gr, gnb, rt, rd: (layer, ge[g], 0, nc + chunk(g, c, gnb))),
            pl.BlockSpec((None, None, tf, d), lambda g, c, ge, gr, gnb, rt, rd: (layer, ge[g], chunk(g, c, gnb), 0)),
            pl.BlockSpec((None, None, 1, d), lambda g, c, ge, gr, gnb, rt, rd: (layer, ge[g], 0, 0)),
        ],
        out_specs=pl.BlockSpec(memory_space=pl.ANY),
        scratch_shapes=[
            pltpu.VMEM((rows * nl, LANES), F32),
            pltpu.VMEM((rows, d), BF16),
            pltpu.VMEM((rows, d), F32),
            pltpu.VMEM((rows * nl, LANES), F32),
            pltpu.SemaphoreType.DMA(()),
            pltpu.SemaphoreType.DMA(()),
        ],
    )
    depth = w_gu.shape[0]
    return pl.pallas_call(
        _moe_kernel,
        grid_spec=grid_spec,
        out_shape=jax.ShapeDtypeStruct((n_slots * nt * nl, LANES), F32),
        compiler_params=_cparams(("arbitrary", "arbitrary")),
        name="moe_experts",
    )(grp_e, grp_r0, grp_nb, row_tok, row_dst, h2, w_gu, w_gu,
      b_gu.reshape(depth, ne, 1, 2 * d_ff), b_gu.reshape(depth, ne, 1, 2 * d_ff),
      w_dn, b_dn.reshape(depth, ne, 1, d))


def _moe_routing(logits, n_experts):
    n = logits.shape[0]
    top_v, top_e = lax.top_k(logits, TOP_K)
    top_w = jax.nn.softmax(top_v, axis=-1)
    e_flat = top_e.reshape(-1).astype(jnp.int32)
    order = jnp.argsort(e_flat).astype(jnp.int32)
    counts = jnp.sum((e_flat[:, None] == jnp.arange(n_experts, dtype=jnp.int32)[None, :]).astype(jnp.int32), axis=0)
    start = jnp.cumsum(counts) - counts
    nblk = (counts + MOE_BLOCK - 1) // MOE_BLOCK
    p_end = jnp.cumsum(nblk * MOE_BLOCK)
    p_start = p_end - nblk * MOE_BLOCK
    n_blk_max = -(-(n * TOP_K) // MOE_BLOCK) + n_experts
    rows = n_blk_max * MOE_BLOCK + MOE_GROUP_BLOCKS * MOE_BLOCK
    rid = jnp.arange(rows, dtype=jnp.int32)
    e_r = jnp.minimum(jnp.sum((rid[:, None] >= p_end[None, :]).astype(jnp.int32), axis=1), n_experts - 1)
    idx = rid - p_start[e_r]
    live = idx < counts[e_r]
    flat = order[jnp.clip(start[e_r] + idx, 0, n * TOP_K - 1)]
    row_tok = jnp.where(live, flat // TOP_K, n - 1)
    row_dst = jnp.where(live, (flat % TOP_K) * n + flat // TOP_K, -1)
    gb = MOE_GROUP_BLOCKS
    ngrp_e = (nblk + gb - 1) // gb
    g_end = jnp.cumsum(ngrp_e)
    n_groups = n_blk_max // gb + n_experts
    gid = jnp.arange(n_groups, dtype=jnp.int32)
    valid = gid < g_end[-1]
    ge = jnp.minimum(jnp.searchsorted(g_end, gid, side='right'), n_experts - 1).astype(jnp.int32)
    last_e = jnp.minimum(jnp.searchsorted(g_end, g_end[-1] - 1, side='right'), n_experts - 1).astype(jnp.int32)
    ge = jnp.where(valid, ge, last_e)
    k_in = gid - (g_end[ge] - ngrp_e[ge])
    gr0 = jnp.where(valid, p_start[ge] + k_in * gb * MOE_BLOCK, 0).astype(jnp.int32)
    gnb = jnp.where(valid, jnp.minimum(gb, nblk[ge] - k_in * gb), 0).astype(jnp.int32)
    return top_w, (ge, gr0, gnb, row_tok, row_dst)


def _combine_kernel(x_ref, w_ref, gt_ref, *refs):
    y_refs, o_ref = refs[:-1], refs[-1]
    w = w_ref[...]
    tm, d = x_ref.shape
    nl = d // LANES

    def slot_rows(y_ref):
        return jnp.concatenate([y_ref[pl.ds(j, tm, stride=nl), :] for j in range(nl)], axis=1)

    acc = w[:, 0:1] * slot_rows(y_refs[0])
    for k in range(1, len(y_refs)):
        acc = acc + w[:, k:k + 1] * slot_rows(y_refs[k])
    y3 = acc.reshape(tm // SUBLANES, SUBLANES, d) * gt_ref[...]
    o_ref[...] = x_ref[...] + y3.reshape(tm, d)


def _combine(tok, x, top_w, y_slots, mod, layer, k_gt):
    d = tok.d
    rank = lambda i: (i, 0)
    slot = lambda k: pl.BlockSpec((NORM_TM * (d // LANES), LANES), lambda i: (k * tok.n_tiles + i, 0))
    return pl.pallas_call(
        _combine_kernel,
        grid=(tok.n_tiles,),
        in_specs=[
            pl.BlockSpec((NORM_TM, d), lambda i: (i, 0)),
            pl.BlockSpec((NORM_TM, TOP_K), lambda i: (i, 0)),
            tok.mod_spec(layer, k_gt, d, rank),
        ] + [slot(k) for k in range(TOP_K)],
        out_specs=pl.BlockSpec((NORM_TM, d), lambda i: (i, 0)),
        out_shape=jax.ShapeDtypeStruct((tok.nt, d), F32),
        compiler_params=_cparams(("arbitrary",)),
        name="moe_combine",
    )(x, top_w, mod, *([y_slots] * TOP_K))


def kernel(x_prompt, x_sample, cache_k, cache_v, state_hgrn, state_conv, page_table, c_prompt, c_sample,
           w_ada, b_ada, g_norm_mix, w_in, hgrn_lb_logits, g_hgrn_out, conv_w, w_br_a, w_br_b, w_br_c, w_o,
           g_norm_ffn, w_router, b_router, w_gu, b_gu, w_dn, b_dn, g_final):
    b, t, d = x_prompt.shape
    db, ts, _ = x_sample.shape
    depth = w_ada.shape[0]
    heads_a, dk_a = state_hgrn.shape[2], state_hgrn.shape[3]
    heads_b, dh_b = cache_k.shape[3], cache_k.shape[4]
    w_a, w_b, w_c = heads_a * dk_a, heads_b * dh_b, state_conv.shape[3]
    n_experts = w_router.shape[-1]
    tok = _Tok(b, t, db, ts, d)
    ntp = tok.ntp

    off_b = 4 * w_a
    off_c = off_b + 3 * w_b
    off_g = off_c + 3 * w_c
    assert w_in.shape[-1] == off_g + 3 * d

    x = jnp.concatenate([x_prompt.reshape(ntp, d), x_sample.reshape(db * ts, d)], axis=0)
    c_rows = jnp.concatenate([jnp.repeat(c_prompt, tok.rep, axis=0), c_sample], axis=0)
    mod = _ada(c_rows, w_ada, b_ada)
    mod = mod.reshape(depth, mod.shape[1], 1, 6 * d)

    lb_soft = jax.nn.softmax(hgrn_lb_logits.astype(F32), axis=0)
    lower_bounds = jnp.cumsum(lb_soft, axis=0) - lb_soft[0:1]
    slopes = jnp.asarray(np.exp2((np.arange(heads_b, dtype=np.float32) + 1.0) * (-8.0 / heads_b)), F32)

    ks_p, vs_p, ks_s, vs_s, ss_p, ss_s, bs_p, bs_s = [], [], [], [], [], [], [], []
    for l in range(depth):
        h = _norm(tok, x, g_norm_mix, mod, l, 1, 0, BF16)
        proj = _mm(h, w_in, l, F32)

        k_new = proj[:, off_b + w_b:off_b + 2 * w_b]
        v_new = proj[:, off_b + 2 * w_b:off_b + 3 * w_b]
        kp4 = k_new[:ntp].reshape(b, t, heads_b, dh_b)
        vp4 = v_new[:ntp].reshape(b, t, heads_b, dh_b)
        ks4 = k_new[ntp:].reshape(db, ts, heads_b, dh_b)
        vs4 = v_new[ntp:].reshape(db, ts, heads_b, dh_b)

        o_a = jnp.zeros((tok.nt, w_a), BF16)
        o_a, s_p = _hgrn_prompt(proj, lower_bounds, g_hgrn_out, o_a, b, t, heads_a, dk_a, l)
        o_a, s_s = _hgrn_sample(proj, lower_bounds, g_hgrn_out, state_hgrn, o_a, ntp, db, ts, heads_a, dk_a, l)
        o_b = jnp.zeros((tok.nt, w_b), F32)
        o_b = _moba_prompt(proj, slopes, o_b, b, t, heads_b, dh_b,
                           off_b // dh_b, (off_b + w_b) // dh_b, (off_b + 2 * w_b) // dh_b)
        o_b = _moba_sample(proj, ks4, vs4, cache_k, cache_v, page_table, o_b, l, ntp, off_b // w_b)
        o_c = jnp.zeros((tok.nt, w_c), BF16)
        o_c, cb_p = _conv_prompt(proj, conv_w, o_c, b, t, w_c, l, off_c // w_c)
        o_c, cb_s = _conv_sample(proj, conv_w, state_conv, o_c, ntp, db, ts, w_c, l, off_c // w_c)
        merged = _merge(o_a, o_b, o_c, proj, off_g, w_br_a, w_br_b, w_br_c, l, d)
        x = _mm_res(tok, merged, w_o, l, x, mod, 2)

        h2, logits = _norm_router(tok, x, g_norm_ffn, mod, l, 4, 3, w_router, b_router)
        top_w, meta = _moe_routing(logits, n_experts)
        y_slots = _moe(h2, meta, w_gu, b_gu, w_dn, b_dn, l, TOP_K)
        x = _combine(tok, x, top_w, y_slots, mod, l, 5)

        ks_p.append(kp4)
        vs_p.append(vp4)
        ks_s.append(ks4)
        vs_s.append(vs4)
        ss_p.append(s_p)
        ss_s.append(s_s)
        bs_p.append(cb_p)
        bs_s.append(cb_s)

    y = _final_norm(x, g_final)
    return (y[:ntp].reshape(b, t, d), y[ntp:].reshape(db, ts, d),
            jnp.stack(ks_p), jnp.stack(vs_p), jnp.stack(ks_s), jnp.stack(vs_s),
            jnp.stack(ss_p), jnp.stack(ss_s), jnp.stack(bs_p), jnp.stack(bs_s))
```

```python
import functools
import math

import jax
import jax.numpy as jnp
import numpy as np
from jax import lax
from jax.experimental import pallas as pl
from jax.experimental.pallas import tpu as pltpu

F32 = jnp.float32
BF16 = jnp.bfloat16

SUBLANES = 8
LANES = 128

EPS = 1e-6
MOBA_BLOCK = 256
MOBA_TOPK = 3
TOP_K = 4
SWIGLU_LIMIT = 7.0
SWIGLU_ALPHA = 1.702
MOE_BLOCK = 128
CONV_W = 3

HGRN_CHUNK = 16
NORM_TM = 256
MOE_GROUP_BLOCKS = 10
MOE_FF_TILE = 256
VMEM_LIMIT = 56 * 1024 * 1024

NT_DIMS = (((1,), (1,)), ((), ()))
TN_DIMS = (((0,), (0,)), ((), ()))


def _cparams(sem):
    return pltpu.CompilerParams(dimension_semantics=sem, vmem_limit_bytes=VMEM_LIMIT)


def _pick(n, cands):
    for c in cands:
        if n % c == 0:
            return c
    raise ValueError(f"no tile in {cands} divides {n}")


def _layer_vec(v):
    return v.reshape(v.shape[0], 1, v.shape[1])


def _layer_vec_spec(layer, n):
    return pl.BlockSpec((None, 1, n), lambda *_: (layer, 0, 0))


def _split_bf16(x):
    hi = x.astype(BF16)
    lo = (x - hi.astype(F32)).astype(BF16)
    return hi, lo


def _dot3(a, b, dims):
    ah, al = _split_bf16(a)
    bh, bl = _split_bf16(b)
    dg = functools.partial(lax.dot_general, dimension_numbers=dims, preferred_element_type=F32)
    return dg(ah, bh) + dg(al, bh) + dg(ah, bl)


def _ada_kernel(c_ref, w_ref, b_ref, o_ref):
    c = c_ref[...]
    ca = (c * jax.nn.sigmoid(c)).astype(BF16)
    o_ref[0] = jnp.dot(ca, w_ref[0].astype(BF16), preferred_element_type=F32) + b_ref[0]


def _ada(c_rows, w_ada, b_ada):
    depth, d, n6 = w_ada.shape
    r = c_rows.shape[0]
    tn = _pick(n6, (1024, 512, 256, 128))
    return pl.pallas_call(
        _ada_kernel,
        grid=(depth, n6 // tn),
        in_specs=[
            pl.BlockSpec((r, d), lambda l, j: (0, 0)),
            pl.BlockSpec((1, d, tn), lambda l, j: (l, 0, j)),
            pl.BlockSpec((1, 1, tn), lambda l, j: (l, 0, j)),
        ],
        out_specs=pl.BlockSpec((1, r, tn), lambda l, j: (l, 0, j)),
        out_shape=jax.ShapeDtypeStruct((depth, r, n6), F32),
        compiler_params=_cparams(("arbitrary", "arbitrary")),
        name="ada_mod",
    )(c_rows, w_ada, b_ada.reshape(depth, 1, n6))


def _rms(x, g):
    ms = jnp.mean(x * x, axis=-1, keepdims=True)
    return x * lax.rsqrt(ms + EPS) * g


def _modulate(y, sc, sh):
    tm, d = y.shape
    y3 = y.reshape(tm // SUBLANES, SUBLANES, d)
    return (y3 * (1.0 + sc) + sh).reshape(tm, d)


def _norm_kernel(x_ref, g_ref, sc_ref, sh_ref, o_ref):
    h = _modulate(_rms(x_ref[...], g_ref[...]), sc_ref[...], sh_ref[...])
    o_ref[...] = h.astype(o_ref.dtype)


def _norm_router_kernel(x_ref, g_ref, sc_ref, sh_ref, wr_ref, br_ref, h_ref, lg_ref):
    h = _modulate(_rms(x_ref[...], g_ref[...]), sc_ref[...], sh_ref[...])
    tm, d = h.shape
    nl = d // LANES
    for j in range(nl):
        h_ref[pl.ds(j, tm, stride=nl), :] = h[:, j * LANES:(j + 1) * LANES]
    lg_ref[...] = _dot3(h, wr_ref[...], (((1,), (0,)), ((), ()))) + br_ref[...]


def _final_norm_kernel(x_ref, g_ref, o_ref):
    o_ref[...] = _rms(x_ref[...], g_ref[...])


class _Tok:
    def __init__(self, b, t, db, ts, d):
        assert ts == SUBLANES and t % NORM_TM == 0 and (db * ts) % NORM_TM == 0
        self.b, self.t, self.db, self.ts, self.d = b, t, db, ts, d
        self.ntp = b * t
        self.nt = b * t + db * ts
        self.rep = NORM_TM // SUBLANES
        self.n_ptiles = self.ntp // NORM_TM
        self.tiles_per_seq = t // NORM_TM
        self.n_tiles = self.nt // NORM_TM

    def rowblk(self, i):
        return jnp.where(i < self.n_ptiles, i // self.tiles_per_seq, self.b + i - self.n_ptiles)

    def mod_spec(self, layer, k, tn, grid_rank_fn):
        nj = self.d // tn

        def imap(*ids):
            i, j = grid_rank_fn(*ids)
            return (layer, self.rowblk(i), 0, k * nj + j)

        return pl.BlockSpec((None, self.rep, 1, tn), imap)


def _norm(tok, x, g, mod, layer, k_sc, k_sh, out_dtype):
    d = tok.d
    rank = lambda i: (i, 0)
    return pl.pallas_call(
        _norm_kernel,
        grid=(tok.n_tiles,),
        in_specs=[
            pl.BlockSpec((NORM_TM, d), lambda i: (i, 0)),
            _layer_vec_spec(layer, d),
            tok.mod_spec(layer, k_sc, d, rank),
            tok.mod_spec(layer, k_sh, d, rank),
        ],
        out_specs=pl.BlockSpec((NORM_TM, d), lambda i: (i, 0)),
        out_shape=jax.ShapeDtypeStruct((tok.nt, d), out_dtype),
        compiler_params=_cparams(("arbitrary",)),
        name="norm_mod",
    )(x, _layer_vec(g), mod, mod)


def _norm_router(tok, x, g, mod, layer, k_sc, k_sh, w_router, b_router):
    d = tok.d
    ne = w_router.shape[-1]
    rank = lambda i: (i, 0)
    return pl.pallas_call(
        _norm_router_kernel,
        grid=(tok.n_tiles,),
        in_specs=[
            pl.BlockSpec((NORM_TM, d), lambda i: (i, 0)),
            _layer_vec_spec(layer, d),
            tok.mod_spec(layer, k_sc, d, rank),
            tok.mod_spec(layer, k_sh, d, rank),
            pl.BlockSpec((None, d, ne), lambda i: (layer, 0, 0)),
            _layer_vec_spec(layer, ne),
        ],
        out_specs=[
            pl.BlockSpec((NORM_TM * (d // LANES), LANES), lambda i: (i, 0)),
            pl.BlockSpec((NORM_TM, ne), lambda i: (i, 0)),
        ],
        out_shape=[
            jax.ShapeDtypeStruct((tok.nt * (d // LANES), LANES), F32),
            jax.ShapeDtypeStruct((tok.nt, ne), F32),
        ],
        compiler_params=_cparams(("arbitrary",)),
        name="norm_mod_router",
    )(x, _layer_vec(g), mod, mod, w_router, _layer_vec(b_router))


def _final_norm(x, g):
    nt, d = x.shape
    return pl.pallas_call(
        _final_norm_kernel,
        grid=(nt // NORM_TM,),
        in_specs=[
            pl.BlockSpec((NORM_TM, d), lambda i: (i, 0)),
            pl.BlockSpec((1, d), lambda i: (0, 0)),
        ],
        out_specs=pl.BlockSpec((NORM_TM, d), lambda i: (i, 0)),
        out_shape=jax.ShapeDtypeStruct((nt, d), F32),
        compiler_params=_cparams(("arbitrary",)),
        name="final_norm",
    )(x, g.reshape(1, d))


def _mm_kernel(x_ref, w_ref, o_ref, wb_ref):
    @pl.when(pl.program_id(1) == 0)
    def _():
        wb_ref[...] = w_ref[...].astype(BF16)

    o_ref[...] = jnp.dot(x_ref[...], wb_ref[...], preferred_element_type=F32).astype(o_ref.dtype)


def _mm(x, w, layer, out_dtype):
    m, k = x.shape
    n = w.shape[-1]
    tm = _pick(m, (1024, 512, 256))
    tn = _pick(n, (1024, 512, 256, 128))
    return pl.pallas_call(
        _mm_kernel,
        grid=(n // tn, m // tm),
        in_specs=[
            pl.BlockSpec((tm, k), lambda j, i: (i, 0)),
            pl.BlockSpec((None, k, tn), lambda j, i: (layer, 0, j)),
        ],
        out_specs=pl.BlockSpec((tm, tn), lambda j, i: (i, j)),
        out_shape=jax.ShapeDtypeStruct((m, n), out_dtype),
        scratch_shapes=[pltpu.VMEM((k, tn), BF16)],
        compiler_params=_cparams(("arbitrary", "arbitrary")),
        name="proj_in",
    )(x, w)


def _mm_res_kernel(x_ref, w_ref, res_ref, gt_ref, o_ref, wb_ref):
    @pl.when(pl.program_id(1) == 0)
    def _():
        wb_ref[...] = w_ref[...].astype(BF16)

    y = jnp.dot(x_ref[...], wb_ref[...], preferred_element_type=F32)
    tm, tn = y.shape
    y3 = y.reshape(tm // SUBLANES, SUBLANES, tn) * gt_ref[...]
    o_ref[...] = res_ref[...] + y3.reshape(tm, tn)


def _mm_res(tok, x, w, layer, res, mod, k_gt):
    m, k = x.shape
    n = w.shape[-1]
    tn = _pick(n, (1024, 512, 256, 128))
    return pl.pallas_call(
        _mm_res_kernel,
        grid=(n // tn, tok.n_tiles),
        in_specs=[
            pl.BlockSpec((NORM_TM, k), lambda j, i: (i, 0)),
            pl.BlockSpec((None, k, tn), lambda j, i: (layer, 0, j)),
            pl.BlockSpec((NORM_TM, tn), lambda j, i: (i, j)),
            tok.mod_spec(layer, k_gt, tn, lambda j, i: (i, j)),
        ],
        out_specs=pl.BlockSpec((NORM_TM, tn), lambda j, i: (i, j)),
        out_shape=jax.ShapeDtypeStruct((m, n), F32),
        scratch_shapes=[pltpu.VMEM((k, tn), BF16)],
        compiler_params=_cparams(("arbitrary", "arbitrary")),
        name="proj_out_residual",
    )(x, w, res, mod)


def _hgrn_chunk(q, fr, iv, lb, st):
    c, dk = q.shape
    sig = jax.nn.sigmoid(fr)
    lf = jnp.log(lb + (1.0 - lb) * sig)
    kk = (1.0 - lb) * jax.nn.sigmoid(-fr)
    row = lax.broadcasted_iota(jnp.int32, (c, dk), 0)
    b = lf
    sh = 1
    while sh < c:
        b = b + jnp.where(row >= sh, pltpu.roll(b, sh, 0), 0.0)
        sh *= 2
    qe = q * jnp.exp(b)
    o_inter = lax.dot_general(qe.astype(BF16), st.astype(BF16), NT_DIMS, preferred_element_type=F32)
    ngrp = c // SUBLANES
    accs = [jnp.zeros((SUBLANES, iv.shape[1]), F32) for _ in range(ngrp)]
    for s in range(c):
        bs = b[s:s + 1, :]
        ks = kk[s:s + 1, :]
        vs = iv[s:s + 1, :]
        for gi in range(s // SUBLANES, ngrp):
            rs = slice(gi * SUBLANES, (gi + 1) * SUBLANES)
            p = q[rs] * jnp.exp(jnp.minimum(b[rs] - bs, 0.0)) * ks
            if gi == s // SUBLANES:
                p = jnp.where(row[rs] >= s, p, 0.0)
            accs[gi] = accs[gi] + jnp.sum(p, axis=-1, keepdims=True) * vs
    o = o_inter + jnp.concatenate(accs, axis=0)
    b_end = b[c - 1:c, :]
    kdec = kk * jnp.exp(b_end - b)
    upd = lax.dot_general(iv.astype(BF16), kdec.astype(BF16), TN_DIMS, preferred_element_type=F32)
    st_new = st * jnp.exp(b_end) + upd
    return o, st_new


def _hgrn_out(o, ga, gw):
    o = o * lax.rsqrt(jnp.mean(o * o, axis=-1, keepdims=True) + EPS) * gw
    return o * (ga * jax.nn.sigmoid(ga))


def _hgrn_prompt_kernel(q_ref, f_ref, i_ref, g_ref, lb_ref, gw_ref, o_all_ref, o_ref, s_ref, st_ref,
                        *, heads, dk):
    del o_all_ref
    t = pl.program_id(1)
    tc = q_ref.shape[0]

    @pl.when(t == 0)
    def _():
        st_ref[...] = jnp.zeros_like(st_ref)

    def body(ci, carry):
        r0 = pl.multiple_of(ci * HGRN_CHUNK, HGRN_CHUNK)
        rows = pl.ds(r0, HGRN_CHUNK)
        for h in range(heads):
            cols = slice(h * dk, (h + 1) * dk)
            o, st_new = _hgrn_chunk(q_ref[rows, cols], f_ref[rows, cols], i_ref[rows, cols],
                                    lb_ref[:, cols], st_ref[h])
            st_ref[h] = st_new
            o_ref[rows, cols] = _hgrn_out(o, g_ref[rows, cols], gw_ref[:, cols]).astype(o_ref.dtype)
        return carry

    lax.fori_loop(0, tc // HGRN_CHUNK, body, 0)

    @pl.when(t == pl.num_programs(1) - 1)
    def _():
        for h in range(heads):
            s_ref[0, h] = st_ref[h].T


def _hgrn_prompt(proj, lb, gw, o_all, b, t, heads, dk, layer):
    wa = heads * dk
    tc = _pick(t, (256, 128, 64, 32, 16))
    nt = t // tc
    col = lambda k: pl.BlockSpec((tc, wa), lambda bi, ti: (bi * nt + ti, k))
    vec = _layer_vec_spec(layer, wa)
    return pl.pallas_call(
        functools.partial(_hgrn_prompt_kernel, heads=heads, dk=dk),
        grid=(b, nt),
        in_specs=[col(0), col(1), col(2), col(3), vec, vec, pl.BlockSpec(memory_space=pl.ANY)],
        out_specs=[
            pl.BlockSpec((tc, wa), lambda bi, ti: (bi * nt + ti, 0)),
            pl.BlockSpec((1, heads, dk, dk), lambda bi, ti: (bi, 0, 0, 0)),
        ],
        out_shape=[
            jax.ShapeDtypeStruct(o_all.shape, o_all.dtype),
            jax.ShapeDtypeStruct((b, heads, dk, dk), F32),
        ],
        input_output_aliases={6: 0},
        scratch_shapes=[pltpu.VMEM((heads, dk, dk), F32)],
        compiler_params=_cparams(("arbitrary", "arbitrary")),
        name="hgrn_prompt",
    )(proj, proj, proj, proj, _layer_vec(lb), _layer_vec(gw), o_all)


def _hgrn_sample_kernel(q_ref, f_ref, i_ref, g_ref, lb_ref, gw_ref, s0_ref, o_all_ref, o_ref, s_ref,
                        *, heads, dk, ts):
    del o_all_ref
    sb = q_ref.shape[0] // ts
    for si in range(sb):
        rows = slice(si * ts, (si + 1) * ts)
        for h in range(heads):
            cols = slice(h * dk, (h + 1) * dk)
            o, st_new = _hgrn_chunk(q_ref[rows, cols], f_ref[rows, cols], i_ref[rows, cols],
                                    lb_ref[:, cols], s0_ref[si, h].T)
            s_ref[si, h] = st_new.T
            o_ref[rows, cols] = _hgrn_out(o, g_ref[rows, cols], gw_ref[:, cols]).astype(o_ref.dtype)


def _hgrn_sample(proj, lb, gw, s0, o_all, row0, db, ts, heads, dk, layer):
    wa = heads * dk
    sb = 2
    assert db % sb == 0 and row0 % (sb * ts) == 0
    blk0 = row0 // (sb * ts)
    col = lambda k: pl.BlockSpec((sb * ts, wa), lambda si: (blk0 + si, k))
    vec = _layer_vec_spec(layer, wa)
    return pl.pallas_call(
        functools.partial(_hgrn_sample_kernel, heads=heads, dk=dk, ts=ts),
        grid=(db // sb,),
        in_specs=[col(0), col(1), col(2), col(3), vec, vec,
                  pl.BlockSpec((None, sb, heads, dk, dk), lambda si: (layer, si, 0, 0, 0)),
                  pl.BlockSpec(memory_space=pl.ANY)],
        out_specs=[
            pl.BlockSpec((sb * ts, wa), lambda si: (blk0 + si, 0)),
            pl.BlockSpec((sb, heads, dk, dk), lambda si: (si, 0, 0, 0)),
        ],
        out_shape=[
            jax.ShapeDtypeStruct(o_all.shape, o_all.dtype),
            jax.ShapeDtypeStruct((db, heads, dk, dk), F32),
        ],
        input_output_aliases={7: 0},
        compiler_params=_cparams(("arbitrary",)),
        name="hgrn_sample",
    )(proj, proj, proj, proj, _layer_vec(lb), _layer_vec(gw), s0, o_all)


def _topk_select(gates, n_valid):
    nb = len(gates)
    sels = []
    for n in range(nb):
        cnt = jnp.zeros(gates[n].shape, F32)
        for m in range(nb):
            if m == n:
                continue
            ahead = (gates[m] >= gates[n]) if m < n else (gates[m] > gates[n])
            one = 1.0 if n_valid is None else jnp.where(m < n_valid, 1.0, 0.0)
            cnt = cnt + jnp.where(ahead, one, 0.0)
        one = 1.0 if n_valid is None else jnp.where(n < n_valid, 1.0, 0.0)
        sels.append(jnp.where(cnt < float(MOBA_TOPK), one, 0.0))
    return sels


def _moba_prompt_kernel(slope_ref, q_ref, k_ref, v_ref, o_all_ref, o_ref, s_scr, *, nb, scale):
    del o_all_ref
    bs = MOBA_BLOCK
    slope = slope_ref[pl.program_id(1)]
    blk = lambda ref, n: ref[n * bs:(n + 1) * bs, :]
    kmean = jnp.concatenate(
        [jnp.sum(blk(k_ref, n), axis=0, keepdims=True) * (1.0 / bs) for n in range(nb)], axis=0)
    key_i = lax.broadcasted_iota(jnp.int32, (bs, bs), 0)
    qry_i = lax.broadcasted_iota(jnp.int32, (bs, bs), 1)
    dist0 = (qry_i - key_i).astype(F32)
    causal = dist0 >= 0.0
    bias0 = slope * dist0

    for j in range(nb):
        q = blk(q_ref, j)
        qs = (q * scale).astype(BF16)
        sels = None
        if j > MOBA_TOPK:
            gt = _dot3(kmean[0:j, :], q, NT_DIMS)
            sels = _topk_select([gt[n:n + 1, :] for n in range(j)], None)
        m = None
        for n in range(j + 1):
            s = lax.dot_general(blk(k_ref, n).astype(BF16), qs, NT_DIMS, preferred_element_type=F32)
            s = s - (bias0 + slope * float((j - n) * bs))
            if n == j:
                s = jnp.where(causal, s, -jnp.inf)
            elif sels is not None:
                s = jnp.where(sels[n] > 0.0, s, -jnp.inf)
            s_scr[n * bs:(n + 1) * bs, :] = s
            mn = jnp.max(s, axis=0, keepdims=True)
            m = mn if m is None else jnp.maximum(m, mn)
        l = jnp.zeros((1, bs), F32)
        acc = jnp.zeros((q.shape[1], bs), F32)
        for n in range(j + 1):
            p = jnp.exp(blk(s_scr, n) - m)
            l = l + jnp.sum(p, axis=0, keepdims=True)
            acc = acc + lax.dot_general(blk(v_ref, n).astype(BF16), p.astype(BF16), TN_DIMS,
                                        preferred_element_type=F32)
        o_ref[j * bs:(j + 1) * bs, :] = (acc / l).T.astype(o_ref.dtype)


def _moba_prompt(proj, slopes, o_all, b, t, heads, dh, q_col, k_col, v_col):
    assert t % MOBA_BLOCK == 0
    nb = t // MOBA_BLOCK
    seq_blk = lambda col: pl.BlockSpec((t, dh), lambda bi, h: (bi, col + h))
    return pl.pallas_call(
        functools.partial(_moba_prompt_kernel, nb=nb, scale=dh ** -0.5),
        grid=(b, heads),
        in_specs=[pl.BlockSpec(memory_space=pltpu.SMEM), seq_blk(q_col), seq_blk(k_col), seq_blk(v_col),
                  pl.BlockSpec(memory_space=pl.ANY)],
        out_specs=pl.BlockSpec((t, dh), lambda bi, h: (bi, h)),
        out_shape=jax.ShapeDtypeStruct(o_all.shape, o_all.dtype),
        input_output_aliases={4: 0},
        scratch_shapes=[pltpu.VMEM((t, MOBA_BLOCK), F32)],
        compiler_params=_cparams(("arbitrary", "arbitrary")),
        name="moba_prompt",
    )(slopes, proj, proj, proj, o_all)


def _moba_sample_kernel(pt_ref, q_ref, kn_ref, vn_ref, ck_hbm, cv_hbm, o_all_ref, o_ref, kbuf, vbuf, s_scr, sem,
                        *, layer, n_pages, page, heads, dh, ts, scale):
    del o_all_ref
    hq = heads * ts
    pages_per_blk = MOBA_BLOCK // page
    nb = n_pages // pages_per_blk
    past_len = n_pages * page
    seq = pl.program_id(0)
    n_seq = pl.num_programs(0)
    slot = seq % 2

    def page_copies(s_idx, slot_idx):
        cps = []
        for p in range(n_pages):
            pg = pt_ref[s_idx, p]
            cps.append(pltpu.make_async_copy(ck_hbm.at[layer, pg], kbuf.at[slot_idx, p], sem.at[slot_idx, 0]))
            cps.append(pltpu.make_async_copy(cv_hbm.at[layer, pg], vbuf.at[slot_idx, p], sem.at[slot_idx, 1]))
        return cps

    @pl.when(seq == 0)
    def _():
        for cp in page_copies(seq, slot):
            cp.start()

    @pl.when(seq + 1 < n_seq)
    def _():
        for cp in page_copies(seq + 1, 1 - slot):
            cp.start()

    for cp in page_copies(seq, slot):
        cp.wait()
    k_refs = [kbuf.at[slot, p] for p in range(n_pages)]
    v_refs = [vbuf.at[slot, p] for p in range(n_pages)]

    q = q_ref[...]
    qall = jnp.concatenate([q[:, h * dh:(h + 1) * dh] for h in range(heads)], axis=0)
    qs = (qall * scale).astype(BF16)
    lane = lax.broadcasted_iota(jnp.int32, (heads, hq), 1)
    sub = lax.broadcasted_iota(jnp.int32, (heads, hq), 0)
    own_head = (lane // ts) == sub
    tq = (lane % ts).astype(F32)
    slope = jnp.exp2(-(sub + 1).astype(F32) * (8.0 / heads))

    gates = []
    for n in range(nb):
        ksum = jnp.zeros((heads, dh), F32)
        for pp in range(pages_per_blk):
            ksum = ksum + jnp.sum(k_refs[n * pages_per_blk + pp][...], axis=0)
        gates.append(_dot3(ksum * (1.0 / MOBA_BLOCK), qall, NT_DIMS))
    sels = _topk_select(gates, None)

    m = jnp.full((heads, hq), -jnp.inf, F32)
    r_loc = lax.broadcasted_iota(jnp.int32, (page, heads, hq), 0).astype(F32)
    for p in range(n_pages):
        k2 = k_refs[p][...].reshape(page * heads, dh).astype(BF16)
        s = lax.dot_general(k2, qs, NT_DIMS, preferred_element_type=F32).reshape(page, heads, hq)
        dist = (past_len - p * page) + tq[None] - r_loc
        s = s - slope[None] * dist
        s = jnp.where(sels[p // pages_per_blk][None] > 0.0, s, -jnp.inf)
        s_scr[p * page:(p + 1) * page] = s
        m = jnp.maximum(m, jnp.max(s, axis=0))
    kn2 = kn_ref[0].reshape(ts * heads, dh).astype(BF16)
    sn = lax.dot_general(kn2, qs, NT_DIMS, preferred_element_type=F32).reshape(ts, heads, hq)
    tk = lax.broadcasted_iota(jnp.int32, (ts, heads, hq), 0).astype(F32)
    sn = sn - slope[None] * (tq[None] - tk)
    sn = jnp.where(tk <= tq[None], sn, -jnp.inf)
    m = jnp.maximum(m, jnp.max(sn, axis=0))

    acc = jnp.zeros((hq, dh), F32)
    den = jnp.zeros((hq, dh), F32)

    def accumulate(acc, den, s, v2):
        pm = jnp.where(own_head[None], jnp.exp(s - m[None]), 0.0)
        pm2 = pm.reshape(s.shape[0] * heads, hq).astype(BF16)
        acc = acc + lax.dot_general(pm2, v2, TN_DIMS, preferred_element_type=F32)
        den = den + lax.dot_general(pm2, jnp.ones(v2.shape, BF16), TN_DIMS, preferred_element_type=F32)
        return acc, den

    for p in range(n_pages):
        v2 = v_refs[p][...].reshape(page * heads, dh).astype(BF16)
        acc, den = accumulate(acc, den, s_scr[p * page:(p + 1) * page], v2)
    acc, den = accumulate(acc, den, sn, vn_ref[0].reshape(ts * heads, dh).astype(BF16))
    o = acc / den
    for h in range(heads):
        o_ref[:, h * dh:(h + 1) * dh] = o[h * ts:(h + 1) * ts, :].astype(o_ref.dtype)


def _moba_sample(proj, kn4, vn4, cache_k, cache_v, page_table, o_all, layer, row0, q_col):
    db, ts, heads, dh = kn4.shape
    n_pages = page_table.shape[1]
    page = cache_k.shape[2]
    assert MOBA_BLOCK % page == 0 and (n_pages * page) % MOBA_BLOCK == 0 and ts <= MOBA_BLOCK
    assert row0 % ts == 0
    wb = heads * dh

    grid_spec = pltpu.PrefetchScalarGridSpec(
        num_scalar_prefetch=1,
        grid=(db,),
        in_specs=[
            pl.BlockSpec((ts, wb), lambda s, pt: (row0 // ts + s, q_col)),
            pl.BlockSpec((1, ts, heads, dh), lambda s, pt: (s, 0, 0, 0)),
            pl.BlockSpec((1, ts, heads, dh), lambda s, pt: (s, 0, 0, 0)),
            pl.BlockSpec(memory_space=pl.ANY),
            pl.BlockSpec(memory_space=pl.ANY),
            pl.BlockSpec(memory_space=pl.ANY),
        ],
        out_specs=pl.BlockSpec((ts, wb), lambda s, pt: (row0 // ts + s, 0)),
        scratch_shapes=[
            pltpu.VMEM((2, n_pages, page, heads, dh), F32),
            pltpu.VMEM((2, n_pages, page, heads, dh), F32),
            pltpu.VMEM((n_pages * page, heads, heads * ts), F32),
            pltpu.SemaphoreType.DMA((2, 2)),
        ],
    )
    return pl.pallas_call(
        functools.partial(_moba_sample_kernel, layer=layer, n_pages=n_pages, page=page, heads=heads, dh=dh,
                          ts=ts, scale=dh ** -0.5),
        grid_spec=grid_spec,
        out_shape=jax.ShapeDtypeStruct(o_all.shape, o_all.dtype),
        input_output_aliases={6: 0},
        compiler_params=_cparams(("arbitrary",)),
        name="moba_sample",
    )(page_table, proj, kn4, vn4, cache_k, cache_v, o_all)


def _conv_prompt_kernel(b_ref, c_ref, h_ref, w_ref, o_all_ref, o_ref, st_ref, ext_ref):
    del o_all_ref
    t = pl.program_id(1)
    tc = b_ref.shape[0]
    pad = SUBLANES

    @pl.when(t == 0)
    def _():
        ext_ref[0:pad, :] = jnp.zeros((pad, ext_ref.shape[1]), F32)

    @pl.when(t > 0)
    def _():
        ext_ref[0:pad, :] = ext_ref[tc:tc + pad, :]

    ext_ref[pad:pad + tc, :] = c_ref[...] * h_ref[...]
    y = (w_ref[0:1, :] * ext_ref[pad - 2:pad - 2 + tc, :]
         + w_ref[1:2, :] * ext_ref[pad - 1:pad - 1 + tc, :]
         + w_ref[2:3, :] * ext_ref[pad:pad + tc, :])
    o_ref[...] = (b_ref[...] * y).astype(o_ref.dtype)

    @pl.when(t == pl.num_programs(1) - 1)
    def _():
        st_ref[0] = ext_ref[pad + tc - (CONV_W - 1):pad + tc, :]


def _conv_prompt(proj, conv_w, o_all, b, t, wc, layer, col0):
    tc = _pick(t, (256, 128, 64, 32, 16))
    nt = t // tc
    col = lambda k: pl.BlockSpec((tc, wc), lambda bi, ti: (bi * nt + ti, col0 + k))
    return pl.pallas_call(
        _conv_prompt_kernel,
        grid=(b, nt),
        in_specs=[col(0), col(1), col(2),
                  pl.BlockSpec((None, CONV_W, wc), lambda bi, ti: (layer, 0, 0)),
                  pl.BlockSpec(memory_space=pl.ANY)],
        out_specs=[
            pl.BlockSpec((tc, wc), lambda bi, ti: (bi * nt + ti, 0)),
            pl.BlockSpec((1, CONV_W - 1, wc), lambda bi, ti: (bi, 0, 0)),
        ],
        out_shape=[
            jax.ShapeDtypeStruct(o_all.shape, o_all.dtype),
            jax.ShapeDtypeStruct((b, CONV_W - 1, wc), F32),
        ],
        input_output_aliases={4: 0},
        scratch_shapes=[pltpu.VMEM((tc + 2 * SUBLANES, wc), F32)],
        compiler_params=_cparams(("arbitrary", "arbitrary")),
        name="conv_prompt",
    )(proj, proj, proj, conv_w, o_all)


def _conv_sample_kernel(b_ref, c_ref, h_ref, w_ref, buf_ref, o_all_ref, o_ref, st_ref, ext_ref, *, ts):
    del o_all_ref
    rows, wc = b_ref.shape
    ns = rows // ts
    pad = SUBLANES
    u = (c_ref[...] * h_ref[...]).reshape(ns, ts, wc)
    ext_ref[:, pad - (CONV_W - 1):pad, :] = buf_ref[...]
    ext_ref[:, pad:pad + ts, :] = u
    w = w_ref[...]
    y = (w[0:1, :][None] * ext_ref[:, pad - 2:pad - 2 + ts, :]
         + w[1:2, :][None] * ext_ref[:, pad - 1:pad - 1 + ts, :]
         + w[2:3, :][None] * u)
    o_ref[...] = (b_ref[...] * y.reshape(rows, wc)).astype(o_ref.dtype)
    st_ref[...] = ext_ref[:, pad + ts - (CONV_W - 1):pad + ts, :]


def _conv_sample(proj, conv_w, buf, o_all, row0, db, ts, wc, layer, col0):
    sb = _pick(db, (32, 16, 8, 4, 2))
    assert row0 % (sb * ts) == 0 and ts >= CONV_W - 1
    blk0 = row0 // (sb * ts)
    col = lambda k: pl.BlockSpec((sb * ts, wc), lambda si: (blk0 + si, col0 + k))
    return pl.pallas_call(
        functools.partial(_conv_sample_kernel, ts=ts),
        grid=(db // sb,),
        in_specs=[col(0), col(1), col(2),
                  pl.BlockSpec((None, CONV_W, wc), lambda si: (layer, 0, 0)),
                  pl.BlockSpec((None, sb, CONV_W - 1, wc), lambda si: (layer, si, 0, 0)),
                  pl.BlockSpec(memory_space=pl.ANY)],
        out_specs=[
            pl.BlockSpec((sb * ts, wc), lambda si: (blk0 + si, 0)),
            pl.BlockSpec((sb, CONV_W - 1, wc), lambda si: (si, 0, 0)),
        ],
        out_shape=[
            jax.ShapeDtypeStruct(o_all.shape, o_all.dtype),
            jax.ShapeDtypeStruct((db, CONV_W - 1, wc), F32),
        ],
        input_output_aliases={5: 0},
        scratch_shapes=[pltpu.VMEM((sb, SUBLANES + ts, wc), F32)],
        compiler_params=_cparams(("arbitrary",)),
        name="conv_sample",
    )(proj, proj, proj, conv_w, buf, o_all)


def _merge_kernel(oa_ref, ob_ref, oc_ref, ga_ref, gb_ref, gc_ref, wa_ref, wb_ref, wc_ref, o_ref,
                  sa_ref, sb_ref, sc_ref):
    @pl.when(pl.program_id(1) == 0)
    def _():
        sa_ref[...] = wa_ref[...].astype(BF16)
        sb_ref[...] = wb_ref[...].astype(BF16)
        sc_ref[...] = wc_ref[...].astype(BF16)

    def branch(o_r, g_r, w_r):
        y = jnp.dot(o_r[...].astype(BF16), w_r[...], preferred_element_type=F32)
        return jax.nn.sigmoid(g_r[...]) * y

    acc = branch(oa_ref, ga_ref, sa_ref) + branch(ob_ref, gb_ref, sb_ref) + branch(oc_ref, gc_ref, sc_ref)
    o_ref[...] = acc.astype(o_ref.dtype)


def _merge(o_a, o_b, o_c, proj, gate_col0, w_a, w_b, w_c, layer, d):
    m = o_a.shape[0]
    tm = _pick(m, (512, 256))
    tn = _pick(d, (512, 256, 128))
    assert gate_col0 % tn == 0
    g0 = gate_col0 // tn
    nj = d // tn
    act = lambda a: pl.BlockSpec((tm, a.shape[1]), lambda j, i: (i, 0))
    gate = lambda k: pl.BlockSpec((tm, tn), lambda j, i: (i, g0 + k * nj + j))
    wgt = lambda w: pl.BlockSpec((None, w.shape[1], tn), lambda j, i: (layer, 0, j))
    return pl.pallas_call(
        _merge_kernel,
        grid=(nj, m // tm),
        in_specs=[act(o_a), act(o_b), act(o_c), gate(0), gate(1), gate(2), wgt(w_a), wgt(w_b), wgt(w_c)],
        out_specs=pl.BlockSpec((tm, tn), lambda j, i: (i, j)),
        out_shape=jax.ShapeDtypeStruct((m, d), BF16),
        scratch_shapes=[pltpu.VMEM((w.shape[1], tn), BF16) for w in (w_a, w_b, w_c)],
        compiler_params=_cparams(("arbitrary", "arbitrary")),
        name="branch_merge",
    )(o_a, o_b, o_c, proj, proj, proj, w_a, w_b, w_c)


def _moe_kernel(ge_ref, gr0_ref, gnb_ref, gnl_ref, rtok_ref, rdst_ref, h_hbm, wg_ref, wu_ref, bg_ref, bu_ref,
                wd_ref, bd_ref, y_hbm, xf_ref, xb_ref, acc_ref, yo_ref, sem_in, sem_out):
    del ge_ref
    g = pl.program_id(0)
    c = pl.program_id(1)
    n_g = pl.num_programs(0)
    n_c = pl.num_programs(1)
    nb = gnb_ref[g]
    nl = xb_ref.shape[1] // LANES
    max_blocks = xb_ref.shape[0] // MOE_BLOCK
    unroll = 8
    g_next = jnp.minimum(g + 1, n_g - 1)
    has_next = (g + 1 < n_g) & (gnb_ref[g_next] > 0)

    def gather_copy(r, tok):
        return pltpu.make_async_copy(h_hbm.at[pl.ds(tok * nl, nl)], xf_ref.at[pl.ds(r * nl, nl)], sem_in)

    def scatter_copy(r, dst):
        return pltpu.make_async_copy(yo_ref.at[pl.ds(r * nl, nl)], y_hbm.at[pl.ds(dst * nl, nl)], sem_out)

    def issue_gather(grp):
        base = gr0_ref[grp]

        def body(i, carry):
            for u in range(unroll):
                r = i * unroll + u
                gather_copy(r, rtok_ref[base + r]).start()
            return carry

        lax.fori_loop(0, gnb_ref[grp] * (MOE_BLOCK // unroll), body, 0)

    def wait_gather(grp):
        def body(i, carry):
            for u in range(unroll):
                gather_copy(i * unroll + u, 0).wait()
            return carry

        lax.fori_loop(0, gnb_ref[grp] * (MOE_BLOCK // unroll), body, 0)

    def scatter_rows(grp, start):
        base = gr0_ref[grp]

        n_live = gnl_ref[grp]
        n_full = n_live // unroll

        def one(r):
            cp = scatter_copy(r, rdst_ref[base + r] if start else 0)
            cp.start() if start else cp.wait()

        def body(i, carry):
            for u in range(unroll):
                one(i * unroll + u)
            return carry

        def tail(r, carry):
            one(r)
            return carry

        lax.fori_loop(0, n_full, body, 0)
        lax.fori_loop(n_full * unroll, n_live, tail, 0)

    @pl.when((c == 0) & (nb > 0))
    def _():
        @pl.when(g == 0)
        def _():
            issue_gather(g)

        wait_gather(g)

        def to_matmul_layout(bi, carry):
            row0 = pl.multiple_of(bi * MOE_BLOCK, MOE_BLOCK)
            for j in range(nl):
                xb_ref[pl.ds(row0, MOE_BLOCK), j * LANES:(j + 1) * LANES] = xf_ref[
                    pl.ds(row0 * nl + j, MOE_BLOCK, stride=nl), :].astype(BF16)
            return carry

        lax.fori_loop(0, nb, to_matmul_layout, 0)

        def zero(bi, carry):
            xb_ref[pl.ds(pl.multiple_of(bi * MOE_BLOCK, MOE_BLOCK), MOE_BLOCK), :] = jnp.zeros(
                (MOE_BLOCK, xb_ref.shape[1]), BF16)
            return carry

        lax.fori_loop(nb, max_blocks, zero, 0)
        acc_ref[...] = jnp.zeros_like(acc_ref)

        @pl.when(has_next)
        def _():
            issue_gather(g_next)

    @pl.when(nb > 0)
    def _():
        x = xb_ref[...]
        gate = jnp.dot(x, wg_ref[...].astype(BF16), preferred_element_type=F32) + bg_ref[...]
        lin = jnp.dot(x, wu_ref[...].astype(BF16), preferred_element_type=F32) + bu_ref[...]
        gate = jnp.minimum(gate, SWIGLU_LIMIT)
        lin = jnp.clip(lin, -SWIGLU_LIMIT, SWIGLU_LIMIT)
        act = gate * jax.nn.sigmoid(SWIGLU_ALPHA * gate) * (lin + 1.0)
        acc_ref[...] += jnp.dot(act.astype(BF16), wd_ref[...].astype(BF16), preferred_element_type=F32)

    @pl.when((c == n_c - 1) & (nb > 0))
    def _():
        @pl.when(g > 0)
        def _():
            scatter_rows(g - 1, start=False)

        def to_token_layout(bi, carry):
            row0 = pl.multiple_of(bi * MOE_BLOCK, MOE_BLOCK)
            for j in range(nl):
                cols = slice(j * LANES, (j + 1) * LANES)
                yo_ref[pl.ds(row0 * nl + j, MOE_BLOCK, stride=nl), :] = (
                    acc_ref[pl.ds(row0, MOE_BLOCK), cols] + bd_ref[:, cols])
            return carry

        lax.fori_loop(0, nb, to_token_layout, 0)
        scatter_rows(g, start=True)

        @pl.when(jnp.logical_not(has_next))
        def _():
            scatter_rows(g, start=False)


def _moe(h2, meta, w_gu, b_gu, w_dn, b_dn, layer, n_slots):
    grp_e, grp_r0, grp_nb, grp_nl, row_tok, row_dst = meta
    d = w_dn.shape[3]
    nl = d // LANES
    nt = h2.shape[0] // nl
    d_ff = w_dn.shape[2]
    ne = w_gu.shape[1]
    tf = _pick(d_ff, (MOE_FF_TILE, 128))
    nc = d_ff // tf
    ng = grp_e.shape[0]
    rows = MOE_GROUP_BLOCKS * MOE_BLOCK

    def chunk(g, c, gnb):
        return jnp.where(gnb[g] > 0, c, nc - 1)

    def wspec(block, imap):
        return pl.BlockSpec((None,) + block,
                            lambda g, c, ge, gr, gnb, *_: (layer,) + imap(ge[g], chunk(g, c, gnb)))

    grid_spec = pltpu.PrefetchScalarGridSpec(
        num_scalar_prefetch=6,
        grid=(ng, nc),
        in_specs=[
            pl.BlockSpec(memory_space=pl.ANY),
            wspec((None, d, tf), lambda e, k: (e, 0, k)),
            wspec((None, d, tf), lambda e, k: (e, 0, nc + k)),
            wspec((None, 1, tf), lambda e, k: (e, 0, k)),
            wspec((None, 1, tf), lambda e, k: (e, 0, nc + k)),
            wspec((None, tf, d), lambda e, k: (e, k, 0)),
            wspec((None, 1, d), lambda e, k: (e, 0, 0)),
        ],
        out_specs=pl.BlockSpec(memory_space=pl.ANY),
        scratch_shapes=[
            pltpu.VMEM((rows * nl, LANES), F32),
            pltpu.VMEM((rows, d), BF16),
            pltpu.VMEM((rows, d), F32),
            pltpu.VMEM((rows * nl, LANES), F32),
            pltpu.SemaphoreType.DMA(()),
            pltpu.SemaphoreType.DMA(()),
        ],
    )
    depth = w_gu.shape[0]
    return pl.pallas_call(
        _moe_kernel,
        grid_spec=grid_spec,
        out_shape=jax.ShapeDtypeStruct((n_slots * nt * nl, LANES), F32),
        compiler_params=_cparams(("arbitrary", "arbitrary")),
        name="moe_experts",
    )(grp_e, grp_r0, grp_nb, grp_nl, row_tok, row_dst, h2, w_gu, w_gu,
      b_gu.reshape(depth, ne, 1, 2 * d_ff), b_gu.reshape(depth, ne, 1, 2 * d_ff),
      w_dn, b_dn.reshape(depth, ne, 1, d))


def _moe_routing(logits, n_experts):
    n = logits.shape[0]
    top_v, top_e = lax.top_k(logits, TOP_K)
    top_w = jax.nn.softmax(top_v, axis=-1)
    e_flat = top_e.reshape(-1).astype(jnp.int32)
    order = jnp.argsort(e_flat).astype(jnp.int32)
    counts = jnp.sum((e_flat[:, None] == jnp.arange(n_experts, dtype=jnp.int32)[None, :]).astype(jnp.int32), axis=0)
    start = jnp.cumsum(counts) - counts
    nblk = (counts + MOE_BLOCK - 1) // MOE_BLOCK
    p_end = jnp.cumsum(nblk * MOE_BLOCK)
    p_start = p_end - nblk * MOE_BLOCK
    n_blk_max = -(-(n * TOP_K) // MOE_BLOCK) + n_experts
    rows = n_blk_max * MOE_BLOCK + MOE_GROUP_BLOCKS * MOE_BLOCK
    rid = jnp.arange(rows, dtype=jnp.int32)
    e_r = jnp.minimum(jnp.sum((rid[:, None] >= p_end[None, :]).astype(jnp.int32), axis=1), n_experts - 1)
    idx = rid - p_start[e_r]
    live = idx < counts[e_r]
    flat = order[jnp.clip(start[e_r] + idx, 0, n * TOP_K - 1)]
    row_tok = jnp.where(live, flat // TOP_K, n - 1)
    row_dst = jnp.where(live, (flat % TOP_K) * n + flat // TOP_K, -1)
    gb = MOE_GROUP_BLOCKS
    ngrp_e = (nblk + gb - 1) // gb
    g_end = jnp.cumsum(ngrp_e)
    n_groups = n_blk_max // gb + n_experts
    gid = jnp.arange(n_groups, dtype=jnp.int32)
    valid = gid < g_end[-1]
    ge = jnp.minimum(jnp.searchsorted(g_end, gid, side='right'), n_experts - 1).astype(jnp.int32)
    last_e = jnp.minimum(jnp.searchsorted(g_end, g_end[-1] - 1, side='right'), n_experts - 1).astype(jnp.int32)
    ge = jnp.where(valid, ge, last_e)
    k_in = gid - (g_end[ge] - ngrp_e[ge])
    gr0 = jnp.where(valid, p_start[ge] + k_in * gb * MOE_BLOCK, 0).astype(jnp.int32)
    gnb = jnp.where(valid, jnp.minimum(gb, nblk[ge] - k_in * gb), 0).astype(jnp.int32)
    gnl = jnp.where(valid, jnp.clip(counts[ge] - k_in * gb * MOE_BLOCK, 0, gnb * MOE_BLOCK), 0).astype(jnp.int32)
    return top_w, (ge, gr0, gnb, gnl, row_tok, row_dst)


def _combine_kernel(x_ref, w_ref, gt_ref, *refs):
    y_refs, o_ref = refs[:-1], refs[-1]
    w = w_ref[...]
    tm, d = x_ref.shape
    nl = d // LANES

    def slot_rows(y_ref):
        return jnp.concatenate([y_ref[pl.ds(j, tm, stride=nl), :] for j in range(nl)], axis=1)

    acc = w[:, 0:1] * slot_rows(y_refs[0])
    for k in range(1, len(y_refs)):
        acc = acc + w[:, k:k + 1] * slot_rows(y_refs[k])
    y3 = acc.reshape(tm // SUBLANES, SUBLANES, d) * gt_ref[...]
    o_ref[...] = x_ref[...] + y3.reshape(tm, d)


def _combine(tok, x, top_w, y_slots, mod, layer, k_gt):
    d = tok.d
    rank = lambda i: (i, 0)
    slot = lambda k: pl.BlockSpec((NORM_TM * (d // LANES), LANES), lambda i: (k * tok.n_tiles + i, 0))
    return pl.pallas_call(
        _combine_kernel,
        grid=(tok.n_tiles,),
        in_specs=[
            pl.BlockSpec((NORM_TM, d), lambda i: (i, 0)),
            pl.BlockSpec((NORM_TM, TOP_K), lambda i: (i, 0)),
            tok.mod_spec(layer, k_gt, d, rank),
        ] + [slot(k) for k in range(TOP_K)],
        out_specs=pl.BlockSpec((NORM_TM, d), lambda i: (i, 0)),
        out_shape=jax.ShapeDtypeStruct((tok.nt, d), F32),
        compiler_params=_cparams(("arbitrary",)),
        name="moe_combine",
    )(x, top_w, mod, *([y_slots] * TOP_K))


def kernel(x_prompt, x_sample, cache_k, cache_v, state_hgrn, state_conv, page_table, c_prompt, c_sample,
           w_ada, b_ada, g_norm_mix, w_in, hgrn_lb_logits, g_hgrn_out, conv_w, w_br_a, w_br_b, w_br_c, w_o,
           g_norm_ffn, w_router, b_router, w_gu, b_gu, w_dn, b_dn, g_final):
    b, t, d = x_prompt.shape
    db, ts, _ = x_sample.shape
    depth = w_ada.shape[0]
    heads_a, dk_a = state_hgrn.shape[2], state_hgrn.shape[3]
    heads_b, dh_b = cache_k.shape[3], cache_k.shape[4]
    w_a, w_b, w_c = heads_a * dk_a, heads_b * dh_b, state_conv.shape[3]
    n_experts = w_router.shape[-1]
    tok = _Tok(b, t, db, ts, d)
    ntp = tok.ntp

    off_b = 4 * w_a
    off_c = off_b + 3 * w_b
    off_g = off_c + 3 * w_c
    assert w_in.shape[-1] == off_g + 3 * d

    x = jnp.concatenate([x_prompt.reshape(ntp, d), x_sample.reshape(db * ts, d)], axis=0)
    c_rows = jnp.concatenate([jnp.repeat(c_prompt, tok.rep, axis=0), c_sample], axis=0)
    mod = _ada(c_rows, w_ada, b_ada)
    mod = mod.reshape(depth, mod.shape[1], 1, 6 * d)

    lb_soft = jax.nn.softmax(hgrn_lb_logits.astype(F32), axis=0)
    lower_bounds = jnp.cumsum(lb_soft, axis=0) - lb_soft[0:1]
    slopes = jnp.asarray(np.exp2((np.arange(heads_b, dtype=np.float32) + 1.0) * (-8.0 / heads_b)), F32)

    ks_p, vs_p, ks_s, vs_s, ss_p, ss_s, bs_p, bs_s = [], [], [], [], [], [], [], []
    for l in range(depth):
        h = _norm(tok, x, g_norm_mix, mod, l, 1, 0, BF16)
        proj = _mm(h, w_in, l, F32)

        k_new = proj[:, off_b + w_b:off_b + 2 * w_b]
        v_new = proj[:, off_b + 2 * w_b:off_b + 3 * w_b]
        kp4 = k_new[:ntp].reshape(b, t, heads_b, dh_b)
        vp4 = v_new[:ntp].reshape(b, t, heads_b, dh_b)
        ks4 = k_new[ntp:].reshape(db, ts, heads_b, dh_b)
        vs4 = v_new[ntp:].reshape(db, ts, heads_b, dh_b)

        o_a = jnp.zeros((tok.nt, w_a), BF16)
        o_a, s_p = _hgrn_prompt(proj, lower_bounds, g_hgrn_out, o_a, b, t, heads_a, dk_a, l)
        o_a, s_s = _hgrn_sample(proj, lower_bounds, g_hgrn_out, state_hgrn, o_a, ntp, db, ts, heads_a, dk_a, l)
        o_b = jnp.zeros((tok.nt, w_b), F32)
        o_b = _moba_prompt(proj, slopes, o_b, b, t, heads_b, dh_b,
                           off_b // dh_b, (off_b + w_b) // dh_b, (off_b + 2 * w_b) // dh_b)
        o_b = _moba_sample(proj, ks4, vs4, cache_k, cache_v, page_table, o_b, l, ntp, off_b // w_b)
        o_c = jnp.zeros((tok.nt, w_c), BF16)
        o_c, cb_p = _conv_prompt(proj, conv_w, o_c, b, t, w_c, l, off_c // w_c)
        o_c, cb_s = _conv_sample(proj, conv_w, state_conv, o_c, ntp, db, ts, w_c, l, off_c // w_c)
        merged = _merge(o_a, o_b, o_c, proj, off_g, w_br_a, w_br_b, w_br_c, l, d)
        x = _mm_res(tok, merged, w_o, l, x, mod, 2)

        h2, logits = _norm_router(tok, x, g_norm_ffn, mod, l, 4, 3, w_router, b_router)
        top_w, meta = _moe_routing(logits, n_experts)
        y_slots = _moe(h2, meta, w_gu, b_gu, w_dn, b_dn, l, TOP_K)
        x = _combine(tok, x, top_w, y_slots, mod, l, 5)

        ks_p.append(kp4)
        vs_p.append(vp4)
        ks_s.append(ks4)
        vs_s.append(vs4)
        ss_p.append(s_p)
        ss_s.append(s_s)
        bs_p.append(cb_p)
        bs_s.append(cb_s)

    y = _final_norm(x, g_final)
    return (y[:ntp].reshape(b, t, d), y[ntp:].reshape(db, ts, d),
            jnp.stack(ks_p), jnp.stack(vs_p), jnp.stack(ks_s), jnp.stack(vs_s),
            jnp.stack(ss_p), jnp.stack(ss_s), jnp.stack(bs_p), jnp.stack(bs_s))
```
